```python
import jax, jax.numpy as jnp
from jax import lax
import numpy as np

D_MODEL = 2048
BATCH = 4
SEQ = 4096
DEPTH = 1

MEM_TOKENS = 256
HEAD_DIM = 64
N_Q_HEADS = 16
N_KV_HEADS = 4
Q_PER_KV = N_Q_HEADS // N_KV_HEADS
ATTN_WIDTH = N_Q_HEADS * HEAD_DIM
KV_WIDTH = N_KV_HEADS * HEAD_DIM
WINDOW = 128
BLOCK = 128
ROPE_THETA = 10000.0
CONV_WIDTH = 1024
CONV_K = 3
X_HEADS = 4
X_HEAD_DIM = 128
X_WIDTH = X_HEADS * X_HEAD_DIM
FFN_HIDDEN = -(-(8 * D_MODEL) // (3 * 256)) * 256
EPS = 1e-6
IN_SIZES = (ATTN_WIDTH, KV_WIDTH, KV_WIDTH, CONV_WIDTH, CONV_WIDTH, CONV_WIDTH, D_MODEL, D_MODEL)
IN_WIDTH = ATTN_WIDTH + 2 * KV_WIDTH + 3 * CONV_WIDTH + 2 * D_MODEL

kernel_name = "hybrid_gated_swa_shortconv_xattn_block"


def rms_norm(x, g):
    xf = x.astype(jnp.float32)
    y = xf * lax.rsqrt(jnp.mean(xf * xf, axis=-1, keepdims=True) + EPS)
    return (y * g.astype(jnp.float32)).astype(x.dtype)


def rope(x, positions):
    half = HEAD_DIM // 2
    inv_freq = ROPE_THETA ** (-jnp.arange(half, dtype=jnp.float32) / half)
    ang = positions.astype(jnp.float32)[:, None] * inv_freq[None, :]
    cos = jnp.cos(ang)[None, :, None, :]
    sin = jnp.sin(ang)[None, :, None, :]
    xf = x.astype(jnp.float32)
    x1, x2 = xf[..., :half], xf[..., half:]
    out = jnp.concatenate([x1 * cos - x2 * sin, x2 * cos + x1 * sin], axis=-1)
    return out.astype(x.dtype)


def _with_prev_block(t):
    pad = [(0, 0)] * t.ndim
    pad[1] = (1, 0)
    prev = jnp.pad(t, pad)[:, :-1]
    return jnp.concatenate([prev, t], axis=2)


def sliding_window_attention(q, k, v, sinks):
    b, t = q.shape[0], q.shape[1]
    nb = t // BLOCK
    qb = q.reshape(b, nb, BLOCK, N_KV_HEADS, Q_PER_KV, HEAD_DIM)
    kband = _with_prev_block(k.reshape(b, nb, BLOCK, N_KV_HEADS, HEAD_DIM))
    vband = _with_prev_block(v.reshape(b, nb, BLOCK, N_KV_HEADS, HEAD_DIM))
    scale = HEAD_DIM ** -0.5
    s = jnp.einsum('bnqhgd,bnkhd->bnhgqk', qb, kband).astype(jnp.float32) * scale
    blk = jnp.arange(nb)[:, None]
    q_pos = blk * BLOCK + jnp.arange(BLOCK)[None, :]
    k_pos = (blk - 1) * BLOCK + jnp.arange(2 * BLOCK)[None, :]
    diff = q_pos[:, :, None] - k_pos[:, None, :]
    valid = (diff >= 0) & (diff < WINDOW) & (k_pos[:, None, :] >= 0)
    s = jnp.where(valid[None, :, None, None, :, :], s, -jnp.inf)
    sink = sinks.astype(jnp.float32).reshape(N_KV_HEADS, Q_PER_KV)[None, None, :, :, None, None]
    m = jnp.maximum(jnp.max(s, axis=-1, keepdims=True), sink)
    p = jnp.exp(s - m)
    p = p / (jnp.sum(p, axis=-1, keepdims=True) + jnp.exp(sink - m))
    o = jnp.einsum('bnhgqk,bnkhd->bnqhgd', p.astype(v.dtype), vband)
    return o.reshape(b, t, ATTN_WIDTH)


def short_gated_conv(z, gate_b, gate_c, conv_w):
    t = z.shape[1]
    cz = gate_c * z
    zp = jnp.pad(cz, ((0, 0), (CONV_K - 1, 0), (0, 0)))
    y = conv_w[0] * zp[:, 0:t]
    for j in range(1, CONV_K):
        y = y + conv_w[j] * zp[:, j:j + t]
    return gate_b * y


def cross_attention(u, mem_n, w_xq, w_xkv, w_xo):
    b, t = u.shape[0], u.shape[1]
    q = (u @ w_xq).reshape(b, t, X_HEADS, X_HEAD_DIM)
    kv = mem_n @ w_xkv
    k = kv[..., :X_WIDTH].reshape(b, -1, X_HEADS, X_HEAD_DIM)
    v = kv[..., X_WIDTH:].reshape(b, -1, X_HEADS, X_HEAD_DIM)
    s = jnp.einsum('bthd,bmhd->bhtm', q, k).astype(jnp.float32) * (X_HEAD_DIM ** -0.5)
    p = jax.nn.softmax(s, axis=-1)
    o = jnp.einsum('bhtm,bmhd->bthd', p.astype(v.dtype), v).reshape(b, t, X_WIDTH)
    return o @ w_xo


def setup_inputs(seed: int = 0) -> dict:
    key = jax.random.key(seed)
    ks = jax.random.split(key, 20)
    f32 = jnp.float32

    def w(k, shape, fan_in):
        return jax.random.normal(k, shape, f32) * (fan_in ** -0.5)

    def gain(k, shape):
        return 1.0 + 0.01 * jax.random.normal(k, shape, f32)

    return {
        "x": jax.random.normal(ks[0], (BATCH, SEQ, D_MODEL), f32),
        "mem": jax.random.normal(ks[1], (BATCH, MEM_TOKENS, D_MODEL), f32),
        "g_mix": gain(ks[2], (DEPTH, D_MODEL)),
        "w_in": w(ks[3], (DEPTH, D_MODEL, IN_WIDTH), D_MODEL),
        "conv_w": w(ks[4], (DEPTH, CONV_K, CONV_WIDTH), CONV_K),
        "attn_sinks": 0.5 * jax.random.normal(ks[5], (DEPTH, N_Q_HEADS), f32),
        "w_attn_proj": w(ks[6], (DEPTH, ATTN_WIDTH, D_MODEL), ATTN_WIDTH),
        "w_conv_proj": w(ks[7], (DEPTH, CONV_WIDTH, D_MODEL), CONV_WIDTH),
        "w_mix_out": w(ks[8], (DEPTH, D_MODEL, D_MODEL), D_MODEL),
        "g_xattn": gain(ks[9], (DEPTH, D_MODEL)),
        "g_mem": gain(ks[10], (DEPTH, D_MODEL)),
        "w_xq": w(ks[11], (DEPTH, D_MODEL, X_WIDTH), D_MODEL),
        "w_xkv": w(ks[12], (DEPTH, D_MODEL, 2 * X_WIDTH), D_MODEL),
        "w_xo": w(ks[13], (DEPTH, X_WIDTH, D_MODEL), X_WIDTH),
        "g_ffn": gain(ks[14], (DEPTH, D_MODEL)),
        "w_ffn_in": w(ks[15], (DEPTH, D_MODEL, 2 * FFN_HIDDEN), D_MODEL),
        "w_ffn_out": w(ks[16], (DEPTH, FFN_HIDDEN, D_MODEL), FFN_HIDDEN),
        "g_final": gain(ks[17], (D_MODEL,)),
    }


def reference(x, mem, g_mix, w_in, conv_w, attn_sinks, w_attn_proj, w_conv_proj, w_mix_out,
              g_xattn, g_mem, w_xq, w_xkv, w_xo, g_ffn, w_ffn_in, w_ffn_out, g_final):
    b, t = x.shape[0], x.shape[1]
    positions = jnp.arange(t, dtype=jnp.int32)
    split_points = np.cumsum(IN_SIZES)[:-1].tolist()
    h = x
    for l in range(DEPTH):
        u = rms_norm(h, g_mix[l])
        proj = u @ w_in[l]
        q, k, v, z, gb, gc, gate_a, gate_c = jnp.split(proj, split_points, axis=-1)
        q = rope(q.reshape(b, t, N_Q_HEADS, HEAD_DIM), positions)
        k = rope(k.reshape(b, t, N_KV_HEADS, HEAD_DIM), positions)
        v = v.reshape(b, t, N_KV_HEADS, HEAD_DIM)
        y_attn = sliding_window_attention(q, k, v, attn_sinks[l]) @ w_attn_proj[l]
        y_conv = short_gated_conv(z, gb, gc, conv_w[l]) @ w_conv_proj[l]
        merged = jax.nn.sigmoid(gate_a) * y_attn + jax.nn.sigmoid(gate_c) * y_conv
        h = h + merged @ w_mix_out[l]
        u = rms_norm(h, g_xattn[l])
        mem_n = rms_norm(mem, g_mem[l])
        h = h + cross_attention(u, mem_n, w_xq[l], w_xkv[l], w_xo[l])
        u = rms_norm(h, g_ffn[l])
        hid = u @ w_ffn_in[l]
        h = h + (jax.nn.silu(hid[..., :FFN_HIDDEN]) * hid[..., FFN_HIDDEN:]) @ w_ffn_out[l]
    return rms_norm(h, g_final)
```

```python
import functools

import jax
import jax.numpy as jnp
import numpy as np
from jax import lax
from jax.experimental import pallas as pl
from jax.experimental.pallas import tpu as pltpu

F32 = jnp.float32
BF16 = jnp.bfloat16

HEAD_DIM = 64
N_Q_HEADS = 16
N_KV_HEADS = 4
Q_PER_KV = N_Q_HEADS // N_KV_HEADS
ATTN_WIDTH = N_Q_HEADS * HEAD_DIM
KV_WIDTH = N_KV_HEADS * HEAD_DIM
WINDOW = 128
ROPE_THETA = 10000.0
CONV_K = 3
X_HEADS = 4
X_HEAD_DIM = 128
EPS = 1e-6

D_MODEL = 2048
CONV_WIDTH = 1024
COL_GATE_A = 0
COL_GATE_C = COL_GATE_A + D_MODEL
COL_Z = COL_GATE_C + D_MODEL
COL_GB = COL_Z + CONV_WIDTH
COL_GC = COL_GB + CONV_WIDTH
COL_Q = COL_GC + CONV_WIDTH
COL_K = COL_Q + ATTN_WIDTH
COL_V = COL_K + KV_WIDTH
IN_WIDTH = COL_V + KV_WIDTH

LANES = 128
HALO = 16
VMEM_LIMIT = 56 * 1024 * 1024


def _rms(x, g):
    ms = jnp.mean(x * x, axis=-1, keepdims=True)
    return x * lax.rsqrt(ms + EPS) * g


def _params(sem):
    return pltpu.CompilerParams(dimension_semantics=sem, vmem_limit_bytes=VMEM_LIMIT)


def _rope_cols(acc, cos, sin_signed, lo, hi):
    lane = lax.broadcasted_iota(jnp.int32, (1, LANES), 1)
    first_half = (lane % HEAD_DIM) < (HEAD_DIM // 2)
    outs = []
    for c in range(acc.shape[1] // LANES):
        xc = acc[:, c * LANES:(c + 1) * LANES]
        if lo <= c * LANES < hi:
            fwd = pltpu.roll(xc, LANES - HEAD_DIM // 2, axis=1)
            bwd = pltpu.roll(xc, HEAD_DIM // 2, axis=1)
            xc = xc * cos + jnp.where(first_half, fwd, bwd) * sin_signed
        outs.append(xc)
    return jnp.concatenate(outs, axis=1)


def _inproj_kernel(x_ref, g_ref, w_ref, cos_ref, sin_ref, o_ref, u_ref, *, tn):
    j = pl.program_id(1)

    @pl.when(j == 0)
    def _():
        u_ref[...] = _rms(x_ref[...], g_ref[...]).astype(BF16)

    acc = jnp.dot(u_ref[...], w_ref[...], preferred_element_type=F32)
    groups = {}
    for jj in range(IN_WIDTH // tn):
        lo = min(max(COL_Q - jj * tn, 0), tn)
        hi = min(max(COL_V - jj * tn, 0), tn)
        groups.setdefault((lo, hi), []).append(jj)
    for (lo, hi), js in groups.items():
        assert js == list(range(js[0], js[-1] + 1))

        @pl.when((j >= js[0]) & (j <= js[-1]))
        def _(lo=lo, hi=hi):
            out = _rope_cols(acc, cos_ref[...], sin_ref[...], lo, hi) if hi > lo else acc
            o_ref[...] = out.astype(o_ref.dtype)


def _inproj(x2, g, w, cos_t, sin_t, seq, *, tm, tn):
    n, d = x2.shape
    width = w.shape[1]
    tpb = seq // tm
    return pl.pallas_call(
        functools.partial(_inproj_kernel, tn=tn),
        grid=(n // tm, width // tn),
        in_specs=[
            pl.BlockSpec((tm, d), lambda i, j: (i, 0)),
            pl.BlockSpec((1, d), lambda i, j: (0, 0)),
            pl.BlockSpec((d, tn), lambda i, j: (0, j)),
            pl.BlockSpec((tm, LANES), lambda i, j: (i % tpb, 0)),
            pl.BlockSpec((tm, LANES), lambda i, j: (i % tpb, 0)),
        ],
        out_specs=pl.BlockSpec((tm, tn), lambda i, j: (i, j)),
        out_shape=jax.ShapeDtypeStruct((n, width), BF16),
        scratch_shapes=[pltpu.VMEM((tm, d), BF16)],
        compiler_params=_params(("parallel", "arbitrary")),
        name="inproj",
    )(x2, g, w, cos_t, sin_t)


def _swa_kernel(sink_ref, q_ref, k_ref, v_ref, kp_ref, vp_ref, o_ref, *, nblk):
    i = pl.program_id(1)
    row = lax.broadcasted_iota(jnp.int32, (WINDOW, 2 * WINDOW), 0)
    col = lax.broadcasted_iota(jnp.int32, (WINDOW, 2 * WINDOW), 1)
    band = (col > row) & (col <= row + WINDOW)
    for b in range(nblk):
        rows = slice(b * WINDOW, (b + 1) * WINDOW)
        if b == 0:
            kband = jnp.concatenate([kp_ref[...], k_ref[rows, :]], axis=0)
            vband = jnp.concatenate([vp_ref[...], v_ref[rows, :]], axis=0)
            valid = band & ((col >= WINDOW) | (i > 0))
        else:
            kband = k_ref[(b - 1) * WINDOW:(b + 1) * WINDOW, :]
            vband = v_ref[(b - 1) * WINDOW:(b + 1) * WINDOW, :]
            valid = band
        qb = q_ref[rows, :]
        for h in range(N_Q_HEADS):
            g = h // Q_PER_KV
            qh = qb[:, h * HEAD_DIM:(h + 1) * HEAD_DIM]
            kh = kband[:, g * HEAD_DIM:(g + 1) * HEAD_DIM]
            vh = vband[:, g * HEAD_DIM:(g + 1) * HEAD_DIM]
            s = lax.dot_general(qh, kh, (((1,), (1,)), ((), ())), preferred_element_type=F32)
            s = jnp.where(valid, s * (HEAD_DIM ** -0.5), -jnp.inf)
            sink = sink_ref[h]
            m = jnp.maximum(jnp.max(s, axis=-1, keepdims=True), sink)
            p = jnp.exp(s - m)
            denom = jnp.sum(p, axis=-1, keepdims=True) + jnp.exp(sink - m)
            o = jnp.dot(p.astype(BF16), vh, preferred_element_type=F32) / denom
            o_ref[rows, h * HEAD_DIM:(h + 1) * HEAD_DIM] = o.astype(o_ref.dtype)


def _swa(proj3, sinks, *, tq):
    b, t, _ = proj3.shape
    nblk = tq // WINDOW
    qcol, kcol, vcol = COL_Q // ATTN_WIDTH, COL_K // KV_WIDTH, COL_V // KV_WIDTH
    return pl.pallas_call(
        functools.partial(_swa_kernel, nblk=nblk),
        grid=(b, t // tq),
        in_specs=[
            pl.BlockSpec(memory_space=pltpu.SMEM),
            pl.BlockSpec((None, tq, ATTN_WIDTH), lambda bi, i: (bi, i, qcol)),
            pl.BlockSpec((None, tq, KV_WIDTH), lambda bi, i: (bi, i, kcol)),
            pl.BlockSpec((None, tq, KV_WIDTH), lambda bi, i: (bi, i, vcol)),
            pl.BlockSpec((None, WINDOW, KV_WIDTH), lambda bi, i: (bi, jnp.maximum(i * nblk - 1, 0), kcol)),
            pl.BlockSpec((None, WINDOW, KV_WIDTH), lambda bi, i: (bi, jnp.maximum(i * nblk - 1, 0), vcol)),
        ],
        out_specs=pl.BlockSpec((None, tq, ATTN_WIDTH), lambda bi, i: (bi, i, 0)),
        out_shape=jax.ShapeDtypeStruct((b, t, ATTN_WIDTH), BF16),
        compiler_params=_params(("parallel", "arbitrary")),
        name="swa",
    )(sinks, proj3, proj3, proj3, proj3, proj3)


def _mix_kernel(attn_ref, z_ref, gb_ref, gc_ref, zp_ref, gcp_ref, ga_ref, gg_ref, x_ref, cw_ref,
                wap_ref, wcp_ref, wmo_ref, o_ref, ext_ref, *, tm, tiles_per_seq):
    i = pl.program_id(0)
    cz = z_ref[...].astype(F32) * gc_ref[...].astype(F32)
    czp = zp_ref[...].astype(F32) * gcp_ref[...].astype(F32)
    czp = jnp.where(i % tiles_per_seq == 0, 0.0, czp)
    ext_ref[0:HALO, :] = czp
    ext_ref[HALO:, :] = cz
    cw = cw_ref[...]
    y = (cw[0:1, :] * ext_ref[HALO - 2:HALO - 2 + tm, :]
         + cw[1:2, :] * ext_ref[HALO - 1:HALO - 1 + tm, :]
         + cw[2:3, :] * cz)
    conv = (gb_ref[...].astype(F32) * y).astype(BF16)
    y_attn = jnp.dot(attn_ref[...], wap_ref[...], preferred_element_type=F32)
    y_conv = jnp.dot(conv, wcp_ref[...], preferred_element_type=F32)
    merged = (jax.nn.sigmoid(ga_ref[...].astype(F32)) * y_attn
              + jax.nn.sigmoid(gg_ref[...].astype(F32)) * y_conv)
    o_ref[...] = x_ref[...] + jnp.dot(merged.astype(BF16), wmo_ref[...], preferred_element_type=F32)


def _mix(attn2, proj, x2, conv_w, wap, wcp, wmo, seq, *, tm):
    n, d = x2.shape
    cwid = wcp.shape[0]
    zc, gbc, gcc = COL_Z // cwid, COL_GB // cwid, COL_GC // cwid
    gac, ggc = COL_GATE_A // d, COL_GATE_C // d
    rb = tm // HALO
    const = lambda i: (0, 0)
    return pl.pallas_call(
        functools.partial(_mix_kernel, tm=tm, tiles_per_seq=seq // tm),
        grid=(n // tm,),
        in_specs=[
            pl.BlockSpec((tm, ATTN_WIDTH), lambda i: (i, 0)),
            pl.BlockSpec((tm, cwid), lambda i: (i, zc)),
            pl.BlockSpec((tm, cwid), lambda i: (i, gbc)),
            pl.BlockSpec((tm, cwid), lambda i: (i, gcc)),
            pl.BlockSpec((HALO, cwid), lambda i: (jnp.maximum(i * rb - 1, 0), zc)),
            pl.BlockSpec((HALO, cwid), lambda i: (jnp.maximum(i * rb - 1, 0), gcc)),
            pl.BlockSpec((tm, d), lambda i: (i, gac)),
            pl.BlockSpec((tm, d), lambda i: (i, ggc)),
            pl.BlockSpec((tm, d), lambda i: (i, 0)),
            pl.BlockSpec((CONV_K, cwid), const),
            pl.BlockSpec(wap.shape, const),
            pl.BlockSpec(wcp.shape, const),
            pl.BlockSpec(wmo.shape, const),
        ],
        out_specs=pl.BlockSpec((tm, d), lambda i: (i, 0)),
        out_shape=jax.ShapeDtypeStruct((n, d), F32),
        scratch_shapes=[pltpu.VMEM((tm + HALO, cwid), F32)],
        compiler_params=_params(("parallel",)),
        name="mix",
    )(attn2, proj, proj, proj, proj, proj, proj, proj, x2, conv_w, wap, wcp, wmo)


def _memkv_kernel(m_ref, g_ref, w_ref, o_ref):
    u = _rms(m_ref[...], g_ref[...]).astype(BF16)
    o_ref[...] = jnp.dot(u, w_ref[...], preferred_element_type=F32).astype(o_ref.dtype)


def _memkv(mem2, g, w, *, tm):
    n, d = mem2.shape
    return pl.pallas_call(
        _memkv_kernel,
        grid=(n // tm,),
        in_specs=[
            pl.BlockSpec((tm, d), lambda i: (i, 0)),
            pl.BlockSpec((1, d), lambda i: (0, 0)),
            pl.BlockSpec(w.shape, lambda i: (0, 0)),
        ],
        out_specs=pl.BlockSpec((tm, w.shape[1]), lambda i: (i, 0)),
        out_shape=jax.ShapeDtypeStruct((n, w.shape[1]), BF16),
        compiler_params=_params(("parallel",)),
        name="memkv",
    )(mem2, g, w)


def _xattn_kernel(h_ref, g_ref, wq_ref, kv_ref, wo_ref, o_ref):
    h = h_ref[...]
    u = _rms(h, g_ref[...]).astype(BF16)
    q = jnp.dot(u, wq_ref[...], preferred_element_type=F32) * (X_HEAD_DIM ** -0.5)
    q = q.astype(BF16)
    xw = X_HEADS * X_HEAD_DIM
    outs = []
    for hd in range(X_HEADS):
        cols = slice(hd * X_HEAD_DIM, (hd + 1) * X_HEAD_DIM)
        kh = kv_ref[:, cols]
        vh = kv_ref[:, xw + hd * X_HEAD_DIM:xw + (hd + 1) * X_HEAD_DIM]
        s = lax.dot_general(q[:, cols], kh, (((1,), (1,)), ((), ())), preferred_element_type=F32)
        m = jnp.max(s, axis=-1, keepdims=True)
        p = jnp.exp(s - m)
        denom = jnp.sum(p, axis=-1, keepdims=True)
        outs.append((jnp.dot(p.astype(BF16), vh, preferred_element_type=F32) / denom).astype(BF16))
    o = jnp.concatenate(outs, axis=1)
    o_ref[...] = h + jnp.dot(o, wo_ref[...], preferred_element_type=F32)


def _xattn(h3, g, wq, kv3, wo, *, tm):
    b, t, d = h3.shape
    m, kvw = kv3.shape[1], kv3.shape[2]
    return pl.pallas_call(
        _xattn_kernel,
        grid=(b, t // tm),
        in_specs=[
            pl.BlockSpec((None, tm, d), lambda bi, i: (bi, i, 0)),
            pl.BlockSpec((1, d), lambda bi, i: (0, 0)),
            pl.BlockSpec(wq.shape, lambda bi, i: (0, 0)),
            pl.BlockSpec((None, m, kvw), lambda bi, i: (bi, 0, 0)),
            pl.BlockSpec(wo.shape, lambda bi, i: (0, 0)),
        ],
        out_specs=pl.BlockSpec((None, tm, d), lambda bi, i: (bi, i, 0)),
        out_shape=jax.ShapeDtypeStruct((b, t, d), F32),
        compiler_params=_params(("parallel", "parallel")),
        name="xattn",
    )(h3, g, wq, kv3, wo)


def _ffn_kernel(h_ref, g_ref, wa_ref, wb_ref, wo_ref, gf_ref, o_ref, u_ref):
    k = pl.program_id(1)

    @pl.when(k == 0)
    def _():
        h = h_ref[...]
        u_ref[...] = _rms(h, g_ref[...]).astype(BF16)
        o_ref[...] = h

    u = u_ref[...]
    a = jnp.dot(u, wa_ref[...], preferred_element_type=F32)
    b = jnp.dot(u, wb_ref[...], preferred_element_type=F32)
    act = (a * jax.nn.sigmoid(a) * b).astype(BF16)
    o_ref[...] += jnp.dot(act, wo_ref[...], preferred_element_type=F32)

    @pl.when(k == pl.num_programs(1) - 1)
    def _():
        o_ref[...] = _rms(o_ref[...], gf_ref[...])


def _ffn(h2, g, w_in, w_out, g_final, *, tm, th):
    n, d = h2.shape
    hidden = w_out.shape[0]
    nk = hidden // th
    return pl.pallas_call(
        _ffn_kernel,
        grid=(n // tm, nk),
        in_specs=[
            pl.BlockSpec((tm, d), lambda i, k: (i, 0)),
            pl.BlockSpec((1, d), lambda i, k: (0, 0)),
            pl.BlockSpec((d, th), lambda i, k: (0, k)),
            pl.BlockSpec((d, th), lambda i, k: (0, nk + k)),
            pl.BlockSpec((th, d), lambda i, k: (k, 0)),
            pl.BlockSpec((1, d), lambda i, k: (0, 0)),
        ],
        out_specs=pl.BlockSpec((tm, d), lambda i, k: (i, 0)),
        out_shape=jax.ShapeDtypeStruct((n, d), F32),
        scratch_shapes=[pltpu.VMEM((tm, d), BF16)],
        compiler_params=_params(("parallel", "arbitrary")),
        name="ffn",
    )(h2, g, w_in, w_in, w_out, g_final)


def _rope_tables(seq):
    half = HEAD_DIM // 2
    inv_freq = ROPE_THETA ** (-jnp.arange(half, dtype=F32) / half)
    ang = jnp.arange(seq, dtype=jnp.int32).astype(F32)[:, None] * inv_freq[None, :]
    cos, sin = jnp.cos(ang), jnp.sin(ang)
    reps = LANES // HEAD_DIM
    cos_t = jnp.tile(jnp.concatenate([cos, cos], axis=1), (1, reps))
    sin_t = jnp.tile(jnp.concatenate([-sin, sin], axis=1), (1, reps))
    return cos_t, sin_t


def _relayout_w_in(w):
    sizes = (ATTN_WIDTH, KV_WIDTH, KV_WIDTH, CONV_WIDTH, CONV_WIDTH, CONV_WIDTH, D_MODEL, D_MODEL)
    q, k, v, z, gb, gc, ga, gg = jnp.split(w.astype(BF16), np.cumsum(sizes)[:-1].tolist(), axis=1)
    return jnp.concatenate([ga, gg, z, gb, gc, q, k, v], axis=1)


def kernel(x, mem, g_mix, w_in, conv_w, attn_sinks, w_attn_proj, w_conv_proj, w_mix_out, g_xattn, g_mem,
           w_xq, w_xkv, w_xo, g_ffn, w_ffn_in, w_ffn_out, g_final):
    b, t, d = x.shape
    depth = w_in.shape[0]
    cos_t, sin_t = _rope_tables(t)
    h = x.reshape(b * t, d)
    mem2 = mem.reshape(b * mem.shape[1], d)
    for l in range(depth):
        proj = _inproj(h, g_mix[l][None, :], _relayout_w_in(w_in[l]), cos_t, sin_t, t, tm=512, tn=512)
        attn = _swa(proj.reshape(b, t, -1), attn_sinks[l], tq=256)
        h = _mix(attn.reshape(b * t, ATTN_WIDTH), proj, h, conv_w[l], w_attn_proj[l].astype(BF16),
                 w_conv_proj[l].astype(BF16), w_mix_out[l].astype(BF16), t, tm=256)
        kv = _memkv(mem2, g_mem[l][None, :], w_xkv[l].astype(BF16), tm=256)
        h = _xattn(h.reshape(b, t, d), g_xattn[l][None, :], w_xq[l].astype(BF16),
                   kv.reshape(b, mem.shape[1], -1), w_xo[l].astype(BF16), tm=512).reshape(b * t, d)
        last = l == depth - 1
        assert last, "fused final norm expects a single layer"
        h = _ffn(h, g_ffn[l][None, :], w_ffn_in[l].astype(BF16), w_ffn_out[l].astype(BF16),
                 g_final[None, :], tm=512, th=512)
    return h.reshape(b, t, d)
```

```python
import functools

import jax
import jax.numpy as jnp
import numpy as np
from jax import lax
from jax.experimental import pallas as pl
from jax.experimental.pallas import tpu as pltpu

F32 = jnp.float32
BF16 = jnp.bfloat16

HEAD_DIM = 64
N_Q_HEADS = 16
N_KV_HEADS = 4
Q_PER_KV = N_Q_HEADS // N_KV_HEADS
ATTN_WIDTH = N_Q_HEADS * HEAD_DIM
KV_WIDTH = N_KV_HEADS * HEAD_DIM
WINDOW = 128
ROPE_THETA = 10000.0
CONV_K = 3
X_HEADS = 4
X_HEAD_DIM = 128
EPS = 1e-6

D_MODEL = 2048
CONV_WIDTH = 1024
COL_GATE_A = 0
COL_GATE_C = COL_GATE_A + D_MODEL
COL_Z = COL_GATE_C + D_MODEL
COL_GB = COL_Z + CONV_WIDTH
COL_GC = COL_GB + CONV_WIDTH
COL_Q = COL_GC + CONV_WIDTH
MAIN_WIDTH = COL_Q + ATTN_WIDTH

LANES = 128
HALO = 16
VMEM_LIMIT = 56 * 1024 * 1024


def _rms(x, g):
    ms = jnp.mean(x * x, axis=-1, keepdims=True)
    return x * lax.rsqrt(ms + EPS) * g


def _params(sem):
    return pltpu.CompilerParams(dimension_semantics=sem, vmem_limit_bytes=VMEM_LIMIT)


def _rope(acc, cos, sin_signed):
    lane = lax.broadcasted_iota(jnp.int32, (1, LANES), 1)
    first_half = (lane % HEAD_DIM) < (HEAD_DIM // 2)
    outs = []
    for c in range(acc.shape[1] // LANES):
        xc = acc[:, c * LANES:(c + 1) * LANES]
        fwd = pltpu.roll(xc, LANES - HEAD_DIM // 2, axis=1)
        bwd = pltpu.roll(xc, HEAD_DIM // 2, axis=1)
        outs.append(xc * cos + jnp.where(first_half, fwd, bwd) * sin_signed)
    return jnp.concatenate(outs, axis=1)


def _norm_kv_kernel(x_ref, g_ref, w_ref, cos_ref, sin_ref, u_ref, kv_ref):
    u = _rms(x_ref[...], g_ref[...]).astype(BF16)
    u_ref[...] = u
    k = jnp.dot(u, w_ref[:, :KV_WIDTH], preferred_element_type=F32)
    kv_ref[:, :KV_WIDTH] = _rope(k, cos_ref[...], sin_ref[...]).astype(kv_ref.dtype)
    kv_ref[:, KV_WIDTH:] = jnp.dot(u, w_ref[:, KV_WIDTH:], preferred_element_type=F32).astype(kv_ref.dtype)


def _norm_kv(x2, g, w_kv, cos_t, sin_t, seq, *, tm):
    n, d = x2.shape
    tpb = seq // tm
    return pl.pallas_call(
        _norm_kv_kernel,
        grid=(n // tm,),
        in_specs=[
            pl.BlockSpec((tm, d), lambda i: (i, 0)),
            pl.BlockSpec((1, d), lambda i: (0, 0)),
            pl.BlockSpec(w_kv.shape, lambda i: (0, 0)),
            pl.BlockSpec((tm, LANES), lambda i: (i % tpb, 0)),
            pl.BlockSpec((tm, LANES), lambda i: (i % tpb, 0)),
        ],
        out_specs=[
            pl.BlockSpec((tm, d), lambda i: (i, 0)),
            pl.BlockSpec((tm, 2 * KV_WIDTH), lambda i: (i, 0)),
        ],
        out_shape=[
            jax.ShapeDtypeStruct((n, d), BF16),
            jax.ShapeDtypeStruct((n, 2 * KV_WIDTH), BF16),
        ],
        compiler_params=_params(("parallel",)),
        name="norm_kv",
    )(x2, g, w_kv, cos_t, sin_t)


def _inproj_kernel(u_ref, w_ref, cos_ref, sin_ref, o_ref, *, tn, chunk):
    j = pl.program_id(0)
    n_tiles = MAIN_WIDTH // tn
    q_tile, q_lo = COL_Q // tn, COL_Q % tn
    assert q_lo % chunk == 0 and MAIN_WIDTH == (q_tile + 1) * tn

    def tile(rope_from):
        for c in range(tn // chunk):
            cols = slice(c * chunk, (c + 1) * chunk)
            acc = jnp.dot(u_ref[...], w_ref[:, cols], preferred_element_type=F32)
            if c * chunk >= rope_from:
                acc = _rope(acc, cos_ref[...], sin_ref[...])
            o_ref[:, cols] = acc.astype(o_ref.dtype)

    pl.when(j < n_tiles - 1)(lambda: tile(tn))
    pl.when(j == n_tiles - 1)(lambda: tile(q_lo))


def _inproj(u, w, cos_t, sin_t, seq, *, tm, tn, chunk):
    n, d = u.shape
    tpb = seq // tm
    return pl.pallas_call(
        functools.partial(_inproj_kernel, tn=tn, chunk=chunk),
        grid=(MAIN_WIDTH // tn, n // tm),
        in_specs=[
            pl.BlockSpec((tm, d), lambda j, i: (i, 0)),
            pl.BlockSpec((d, tn), lambda j, i: (0, j)),
            pl.BlockSpec((tm, LANES), lambda j, i: (i % tpb, 0)),
            pl.BlockSpec((tm, LANES), lambda j, i: (i % tpb, 0)),
        ],
        out_specs=pl.BlockSpec((tm, tn), lambda j, i: (i, j)),
        out_shape=jax.ShapeDtypeStruct((n, MAIN_WIDTH), BF16),
        compiler_params=_params(("parallel", "parallel")),
        name="inproj",
    )(u, w, cos_t, sin_t)


def _swa_kernel(sink_ref, q_ref, k_ref, v_ref, kp_ref, vp_ref, o_ref, bias_ref, *, nblk):
    i = pl.program_id(1)
    stacked = Q_PER_KV * WINDOW
    key = lax.broadcasted_iota(jnp.int32, (2 * WINDOW, stacked), 0)
    qry = lax.broadcasted_iota(jnp.int32, (2 * WINDOW, stacked), 1) % WINDOW
    band = (key > qry) & (key <= qry + WINDOW)
    bias_ref[0] = jnp.where(band & ((key >= WINDOW) | (i > 0)), 0.0, -jnp.inf)
    bias_ref[1] = jnp.where(band, 0.0, -jnp.inf)
    lane_head = lax.broadcasted_iota(jnp.int32, (1, stacked), 1) // WINDOW
    for b in range(nblk):
        rows = slice(b * WINDOW, (b + 1) * WINDOW)
        if b == 0:
            kband = jnp.concatenate([kp_ref[...], k_ref[rows, :]], axis=0)
            vband = jnp.concatenate([vp_ref[...], v_ref[rows, :]], axis=0)
        else:
            kband = k_ref[(b - 1) * WINDOW:(b + 1) * WINDOW, :]
            vband = v_ref[(b - 1) * WINDOW:(b + 1) * WINDOW, :]
        qb = q_ref[rows, :] * (HEAD_DIM ** -0.5)
        for g in range(N_KV_HEADS):
            heads = range(g * Q_PER_KV, (g + 1) * Q_PER_KV)
            qs = jnp.concatenate([qb[:, h * HEAD_DIM:(h + 1) * HEAD_DIM] for h in heads], axis=0)
            kh = kband[:, g * HEAD_DIM:(g + 1) * HEAD_DIM]
            vh = vband[:, g * HEAD_DIM:(g + 1) * HEAD_DIM]
            sink = jnp.zeros((1, stacked), F32)
            for n, h in enumerate(heads):
                sink = jnp.where(lane_head == n, sink_ref[h], sink)
            st = lax.dot_general(kh, qs, (((1,), (1,)), ((), ())), preferred_element_type=F32)
            st = st + bias_ref[0 if b == 0 else 1]
            m = jnp.maximum(jnp.max(st, axis=0, keepdims=True), sink)
            p = jnp.exp(st - m)
            denom = jnp.sum(p, axis=0, keepdims=True) + jnp.exp(sink - m)
            pn = (p * (1.0 / denom)).astype(BF16)
            ot = lax.dot_general(vh, pn, (((0,), (0,)), ((), ())), preferred_element_type=F32)
            for n, h in enumerate(heads):
                o_ref[h * HEAD_DIM:(h + 1) * HEAD_DIM, rows] = ot[:, n * WINDOW:(n + 1) * WINDOW].astype(o_ref.dtype)


def _swa(proj3, kv3, sinks, *, tq):
    b, t, _ = proj3.shape
    nblk = tq // WINDOW
    qcol = COL_Q // ATTN_WIDTH
    prev = lambda bi, i: jnp.maximum(i * nblk - 1, 0)
    return pl.pallas_call(
        functools.partial(_swa_kernel, nblk=nblk),
        grid=(b, t // tq),
        in_specs=[
            pl.BlockSpec(memory_space=pltpu.SMEM),
            pl.BlockSpec((None, tq, ATTN_WIDTH), lambda bi, i: (bi, i, qcol)),
            pl.BlockSpec((None, tq, KV_WIDTH), lambda bi, i: (bi, i, 0)),
            pl.BlockSpec((None, tq, KV_WIDTH), lambda bi, i: (bi, i, 1)),
            pl.BlockSpec((None, WINDOW, KV_WIDTH), lambda bi, i: (bi, prev(bi, i), 0)),
            pl.BlockSpec((None, WINDOW, KV_WIDTH), lambda bi, i: (bi, prev(bi, i), 1)),
        ],
        out_specs=pl.BlockSpec((None, ATTN_WIDTH, tq), lambda bi, i: (bi, 0, i)),
        out_shape=jax.ShapeDtypeStruct((b, ATTN_WIDTH, t), BF16),
        scratch_shapes=[pltpu.VMEM((2, 2 * WINDOW, Q_PER_KV * WINDOW), F32)],
        compiler_params=_params(("parallel", "arbitrary")),
        name="swa",
    )(sinks, proj3, kv3, kv3, kv3, kv3)


def _mix_kernel(attn_ref, z_ref, gb_ref, gc_ref, zp_ref, gcp_ref, ga_ref, gg_ref, x_ref, cw_ref,
                wap_ref, wcp_ref, wmo_ref, o_ref, ext_ref, *, tm, tiles_per_seq):
    i = pl.program_id(0)
    cz = z_ref[...].astype(F32) * gc_ref[...].astype(F32)
    czp = zp_ref[...].astype(F32) * gcp_ref[...].astype(F32)
    czp = jnp.where(i % tiles_per_seq == 0, 0.0, czp)
    ext_ref[0:HALO, :] = czp
    ext_ref[HALO:, :] = cz
    cw = cw_ref[...]
    y = (cw[0:1, :] * ext_ref[HALO - 2:HALO - 2 + tm, :]
         + cw[1:2, :] * ext_ref[HALO - 1:HALO - 1 + tm, :]
         + cw[2:3, :] * cz)
    conv = (gb_ref[...].astype(F32) * y).astype(BF16)
    y_attn = lax.dot_general(attn_ref[...], wap_ref[...], (((0,), (0,)), ((), ())), preferred_element_type=F32)
    y_conv = jnp.dot(conv, wcp_ref[...], preferred_element_type=F32)
    merged = (jax.nn.sigmoid(ga_ref[...].astype(F32)) * y_attn
              + jax.nn.sigmoid(gg_ref[...].astype(F32)) * y_conv)
    o_ref[...] = x_ref[...] + jnp.dot(merged.astype(BF16), wmo_ref[...], preferred_element_type=F32)


def _mix(attn_t, proj, x2, conv_w, wap, wcp, wmo, seq, *, tm):
    n, d = x2.shape
    cwid = wcp.shape[0]
    tps = seq // tm
    zc, gbc, gcc = COL_Z // cwid, COL_GB // cwid, COL_GC // cwid
    gac, ggc = COL_GATE_A // d, COL_GATE_C // d
    rb = tm // HALO
    const = lambda i: (0, 0)
    return pl.pallas_call(
        functools.partial(_mix_kernel, tm=tm, tiles_per_seq=tps),
        grid=(n // tm,),
        in_specs=[
            pl.BlockSpec((None, ATTN_WIDTH, tm), lambda i: (i // tps, 0, i % tps)),
            pl.BlockSpec((tm, cwid), lambda i: (i, zc)),
            pl.BlockSpec((tm, cwid), lambda i: (i, gbc)),
            pl.BlockSpec((tm, cwid), lambda i: (i, gcc)),
            pl.BlockSpec((HALO, cwid), lambda i: (jnp.maximum(i * rb - 1, 0), zc)),
            pl.BlockSpec((HALO, cwid), lambda i: (jnp.maximum(i * rb - 1, 0), gcc)),
            pl.BlockSpec((tm, d), lambda i: (i, gac)),
            pl.BlockSpec((tm, d), lambda i: (i, ggc)),
            pl.BlockSpec((tm, d), lambda i: (i, 0)),
            pl.BlockSpec((CONV_K, cwid), const),
            pl.BlockSpec(wap.shape, const),
            pl.BlockSpec(wcp.shape, const),
            pl.BlockSpec(wmo.shape, const),
        ],
        out_specs=pl.BlockSpec((tm, d), lambda i: (i, 0)),
        out_shape=jax.ShapeDtypeStruct((n, d), F32),
        scratch_shapes=[pltpu.VMEM((tm + HALO, cwid), F32)],
        compiler_params=_params(("parallel",)),
        name="mix",
    )(attn_t, proj, proj, proj, proj, proj, proj, proj, x2, conv_w, wap, wcp, wmo)


def _memkv_kernel(m_ref, g_ref, w_ref, o_ref):
    u = _rms(m_ref[...], g_ref[...]).astype(BF16)
    o_ref[...] = jnp.dot(u, w_ref[...], preferred_element_type=F32).astype(o_ref.dtype)


def _memkv(mem2, g, w, *, tm):
    n, d = mem2.shape
    return pl.pallas_call(
        _memkv_kernel,
        grid=(n // tm,),
        in_specs=[
            pl.BlockSpec((tm, d), lambda i: (i, 0)),
            pl.BlockSpec((1, d), lambda i: (0, 0)),
            pl.BlockSpec(w.shape, lambda i: (0, 0)),
        ],
        out_specs=pl.BlockSpec((tm, w.shape[1]), lambda i: (i, 0)),
        out_shape=jax.ShapeDtypeStruct((n, w.shape[1]), BF16),
        compiler_params=_params(("parallel",)),
        name="memkv",
    )(mem2, g, w)


def _xattn_kernel(h_ref, g_ref, wq_ref, kv_ref, wo_ref, o_ref):
    h = h_ref[...]
    u = _rms(h, g_ref[...]).astype(BF16)
    q = jnp.dot(u, wq_ref[...], preferred_element_type=F32) * (X_HEAD_DIM ** -0.5)
    q = q.astype(BF16)
    xw = X_HEADS * X_HEAD_DIM
    outs = []
    for hd in range(X_HEADS):
        cols = slice(hd * X_HEAD_DIM, (hd + 1) * X_HEAD_DIM)
        kh = kv_ref[:, cols]
        vh = kv_ref[:, xw + hd * X_HEAD_DIM:xw + (hd + 1) * X_HEAD_DIM]
        s = lax.dot_general(q[:, cols], kh, (((1,), (1,)), ((), ())), preferred_element_type=F32)
        m = jnp.max(s, axis=-1, keepdims=True)
        p = jnp.exp(s - m)
        denom = jnp.sum(p, axis=-1, keepdims=True)
        outs.append((jnp.dot(p.astype(BF16), vh, preferred_element_type=F32) / denom).astype(BF16))
    o = jnp.concatenate(outs, axis=1)
    o_ref[...] = h + jnp.dot(o, wo_ref[...], preferred_element_type=F32)


def _xattn(h3, g, wq, kv3, wo, *, tm):
    b, t, d = h3.shape
    m, kvw = kv3.shape[1], kv3.shape[2]
    return pl.pallas_call(
        _xattn_kernel,
        grid=(b, t // tm),
        in_specs=[
            pl.BlockSpec((None, tm, d), lambda bi, i: (bi, i, 0)),
            pl.BlockSpec((1, d), lambda bi, i: (0, 0)),
            pl.BlockSpec(wq.shape, lambda bi, i: (0, 0)),
            pl.BlockSpec((None, m, kvw), lambda bi, i: (bi, 0, 0)),
            pl.BlockSpec(wo.shape, lambda bi, i: (0, 0)),
        ],
        out_specs=pl.BlockSpec((None, tm, d), lambda bi, i: (bi, i, 0)),
        out_shape=jax.ShapeDtypeStruct((b, t, d), F32),
        compiler_params=_params(("parallel", "parallel")),
        name="xattn",
    )(h3, g, wq, kv3, wo)


def _ffn_kernel(h_ref, g_ref, wa_ref, wb_ref, wo_ref, gf_ref, o_ref, u_ref):
    k = pl.program_id(1)

    @pl.when(k == 0)
    def _():
        h = h_ref[...]
        u_ref[...] = _rms(h, g_ref[...]).astype(BF16)
        o_ref[...] = h

    u = u_ref[...]
    a = jnp.dot(u, wa_ref[...], preferred_element_type=F32)
    b = jnp.dot(u, wb_ref[...], preferred_element_type=F32)
    act = (a * jax.nn.sigmoid(a) * b).astype(BF16)
    o_ref[...] += jnp.dot(act, wo_ref[...], preferred_element_type=F32)

    @pl.when(k == pl.num_programs(1) - 1)
    def _():
        o_ref[...] = _rms(o_ref[...], gf_ref[...])


def _ffn(h2, g, w_in, w_out, g_final, *, tm, th):
    n, d = h2.shape
    hidden = w_out.shape[0]
    nk = hidden // th
    return pl.pallas_call(
        _ffn_kernel,
        grid=(n // tm, nk),
        in_specs=[
            pl.BlockSpec((tm, d), lambda i, k: (i, 0)),
            pl.BlockSpec((1, d), lambda i, k: (0, 0)),
            pl.BlockSpec((d, th), lambda i, k: (0, k)),
            pl.BlockSpec((d, th), lambda i, k: (0, nk + k)),
            pl.BlockSpec((th, d), lambda i, k: (k, 0)),
            pl.BlockSpec((1, d), lambda i, k: (0, 0)),
        ],
        out_specs=pl.BlockSpec((tm, d), lambda i, k: (i, 0)),
        out_shape=jax.ShapeDtypeStruct((n, d), F32),
        scratch_shapes=[pltpu.VMEM((tm, d), BF16)],
        compiler_params=_params(("parallel", "arbitrary")),
        name="ffn",
    )(h2, g, w_in, w_in, w_out, g_final)


def _rope_tables(seq):
    half = HEAD_DIM // 2
    inv_freq = ROPE_THETA ** (-jnp.arange(half, dtype=F32) / half)
    ang = jnp.arange(seq, dtype=jnp.int32).astype(F32)[:, None] * inv_freq[None, :]
    cos, sin = jnp.cos(ang), jnp.sin(ang)
    reps = LANES // HEAD_DIM
    cos_t = jnp.tile(jnp.concatenate([cos, cos], axis=1), (1, reps))
    sin_t = jnp.tile(jnp.concatenate([-sin, sin], axis=1), (1, reps))
    return cos_t, sin_t


def _relayout_w_in(w):
    sizes = (ATTN_WIDTH, KV_WIDTH, KV_WIDTH, CONV_WIDTH, CONV_WIDTH, CONV_WIDTH, D_MODEL, D_MODEL)
    q, k, v, z, gb, gc, ga, gg = jnp.split(w.astype(BF16), np.cumsum(sizes)[:-1].tolist(), axis=1)
    return jnp.concatenate([ga, gg, z, gb, gc, q], axis=1), jnp.concatenate([k, v], axis=1)


def kernel(x, mem, g_mix, w_in, conv_w, attn_sinks, w_attn_proj, w_conv_proj, w_mix_out, g_xattn, g_mem,
           w_xq, w_xkv, w_xo, g_ffn, w_ffn_in, w_ffn_out, g_final):
    b, t, d = x.shape
    depth = w_in.shape[0]
    cos_t, sin_t = _rope_tables(t)
    h = x.reshape(b * t, d)
    mem2 = mem.reshape(b * mem.shape[1], d)
    for l in range(depth):
        w_main, w_kv = _relayout_w_in(w_in[l])
        u, kv = _norm_kv(h, g_mix[l][None, :], w_kv, cos_t, sin_t, t, tm=512)
        proj = _inproj(u, w_main, cos_t, sin_t, t, tm=512, tn=2048, chunk=512)
        attn = _swa(proj.reshape(b, t, -1), kv.reshape(b, t, -1), attn_sinks[l], tq=512)
        h = _mix(attn, proj, h, conv_w[l], w_attn_proj[l].astype(BF16),
                 w_conv_proj[l].astype(BF16), w_mix_out[l].astype(BF16), t, tm=256)
        mkv = _memkv(mem2, g_mem[l][None, :], w_xkv[l].astype(BF16), tm=256)
        h = _xattn(h.reshape(b, t, d), g_xattn[l][None, :], w_xq[l].astype(BF16),
                   mkv.reshape(b, mem.shape[1], -1), w_xo[l].astype(BF16), tm=512).reshape(b * t, d)
        last = l == depth - 1
        assert last, "fused final norm expects a single layer"
        h = _ffn(h, g_ffn[l][None, :], w_ffn_in[l].astype(BF16), w_ffn_out[l].astype(BF16),
                 g_final[None, :], tm=512, th=512)
    return h.reshape(b, t, d)
```

```python
import functools

import jax
import jax.numpy as jnp
import numpy as np
from jax import lax
from jax.experimental import pallas as pl
from jax.experimental.pallas import tpu as pltpu

F32 = jnp.float32
BF16 = jnp.bfloat16

HEAD_DIM = 64
N_Q_HEADS = 16
N_KV_HEADS = 4
Q_PER_KV = N_Q_HEADS // N_KV_HEADS
ATTN_WIDTH = N_Q_HEADS * HEAD_DIM
KV_WIDTH = N_KV_HEADS * HEAD_DIM
WINDOW = 128
ROPE_THETA = 10000.0
CONV_K = 3
X_HEADS = 4
X_HEAD_DIM = 128
EPS = 1e-6
LOG2E = 1.4426950408889634

D_MODEL = 2048
CONV_WIDTH = 1024
COL_GATE_A = 0
COL_GATE_C = COL_GATE_A + D_MODEL
COL_Z = COL_GATE_C + D_MODEL
COL_GB = COL_Z + CONV_WIDTH
COL_GC = COL_GB + CONV_WIDTH
COL_Q = COL_GC + CONV_WIDTH
MAIN_WIDTH = COL_Q + ATTN_WIDTH

LANES = 128
HALO = 16
VMEM_LIMIT = 56 * 1024 * 1024


def _rms(x, g):
    ms = jnp.mean(x * x, axis=-1, keepdims=True)
    return x * lax.rsqrt(ms + EPS) * g


def _params(sem):
    return pltpu.CompilerParams(dimension_semantics=sem, vmem_limit_bytes=VMEM_LIMIT)


def _rope(acc, cos, sin_signed):
    lane = lax.broadcasted_iota(jnp.int32, (1, LANES), 1)
    first_half = (lane % HEAD_DIM) < (HEAD_DIM // 2)
    outs = []
    for c in range(acc.shape[1] // LANES):
        xc = acc[:, c * LANES:(c + 1) * LANES]
        fwd = pltpu.roll(xc, LANES - HEAD_DIM // 2, axis=1)
        bwd = pltpu.roll(xc, HEAD_DIM // 2, axis=1)
        outs.append(xc * cos + jnp.where(first_half, fwd, bwd) * sin_signed)
    return jnp.concatenate(outs, axis=1)


def _norm_kv_kernel(x_ref, g_ref, w_ref, cos_ref, sin_ref, u_ref, kv_ref):
    u = _rms(x_ref[...], g_ref[...]).astype(BF16)
    u_ref[...] = u
    k = jnp.dot(u, w_ref[:, :KV_WIDTH], preferred_element_type=F32)
    kv_ref[:, :KV_WIDTH] = _rope(k, cos_ref[...], sin_ref[...]).astype(kv_ref.dtype)
    kv_ref[:, KV_WIDTH:] = jnp.dot(u, w_ref[:, KV_WIDTH:], preferred_element_type=F32).astype(kv_ref.dtype)


def _norm_kv(x2, g, w_kv, cos_t, sin_t, seq, *, tm):
    n, d = x2.shape
    tpb = seq // tm
    return pl.pallas_call(
        _norm_kv_kernel,
        grid=(n // tm,),
        in_specs=[
            pl.BlockSpec((tm, d), lambda i: (i, 0)),
            pl.BlockSpec((1, d), lambda i: (0, 0)),
            pl.BlockSpec(w_kv.shape, lambda i: (0, 0)),
            pl.BlockSpec((tm, LANES), lambda i: (i % tpb, 0)),
            pl.BlockSpec((tm, LANES), lambda i: (i % tpb, 0)),
        ],
        out_specs=[
            pl.BlockSpec((tm, d), lambda i: (i, 0)),
            pl.BlockSpec((tm, 2 * KV_WIDTH), lambda i: (i, 0)),
        ],
        out_shape=[
            jax.ShapeDtypeStruct((n, d), BF16),
            jax.ShapeDtypeStruct((n, 2 * KV_WIDTH), BF16),
        ],
        compiler_params=_params(("parallel",)),
        name="norm_kv",
    )(x2, g, w_kv, cos_t, sin_t)


def _inproj_kernel(u_ref, w_ref, cos_ref, sin_ref, o_ref, *, tn, chunk):
    j = pl.program_id(0)
    n_tiles = MAIN_WIDTH // tn
    q_tile, q_lo = COL_Q // tn, COL_Q % tn
    assert q_lo % chunk == 0 and MAIN_WIDTH == (q_tile + 1) * tn

    def tile(rope_from):
        for c in range(tn // chunk):
            cols = slice(c * chunk, (c + 1) * chunk)
            acc = jnp.dot(u_ref[...], w_ref[:, cols], preferred_element_type=F32)
            if c * chunk >= rope_from:
                acc = _rope(acc, cos_ref[...], sin_ref[...])
            o_ref[:, cols] = acc.astype(o_ref.dtype)

    pl.when(j < n_tiles - 1)(lambda: tile(tn))
    pl.when(j == n_tiles - 1)(lambda: tile(q_lo))


def _inproj(u, w, cos_t, sin_t, seq, *, tm, tn, chunk):
    n, d = u.shape
    tpb = seq // tm
    return pl.pallas_call(
        functools.partial(_inproj_kernel, tn=tn, chunk=chunk),
        grid=(MAIN_WIDTH // tn, n // tm),
        in_specs=[
            pl.BlockSpec((tm, d), lambda j, i: (i, 0)),
            pl.BlockSpec((d, tn), lambda j, i: (0, j)),
            pl.BlockSpec((tm, LANES), lambda j, i: (i % tpb, 0)),
            pl.BlockSpec((tm, LANES), lambda j, i: (i % tpb, 0)),
        ],
        out_specs=pl.BlockSpec((tm, tn), lambda j, i: (i, j)),
        out_shape=jax.ShapeDtypeStruct((n, MAIN_WIDTH), BF16),
        compiler_params=_params(("parallel", "parallel")),
        name="inproj",
    )(u, w, cos_t, sin_t)


def _swa_kernel(sink_ref, q_ref, k_ref, v_ref, kp_ref, vp_ref, o_ref, bias_ref, st_ref, *, nblk):
    i = pl.program_id(1)
    stacked = Q_PER_KV * WINDOW
    key = lax.broadcasted_iota(jnp.int32, (2 * WINDOW, stacked), 0)
    qry = lax.broadcasted_iota(jnp.int32, (2 * WINDOW, stacked), 1) % WINDOW
    band = (key > qry) & (key <= qry + WINDOW)
    bias_ref[0] = jnp.where(band & ((key >= WINDOW) | (i > 0)), 0.0, -jnp.inf)
    bias_ref[1] = jnp.where(band, 0.0, -jnp.inf)
    lane_head = lax.broadcasted_iota(jnp.int32, (1, stacked), 1) // WINDOW

    def band_of(ref, prev_ref, b, g):
        cols = slice(g * HEAD_DIM, (g + 1) * HEAD_DIM)
        if b == 0:
            return jnp.concatenate([prev_ref[:, cols], ref[0:WINDOW, cols]], axis=0)
        return ref[(b - 1) * WINDOW:(b + 1) * WINDOW, cols]

    def scores(b, g):
        rows = slice(b * WINDOW, (b + 1) * WINDOW)
        heads = range(g * Q_PER_KV, (g + 1) * Q_PER_KV)
        qs = jnp.concatenate([q_ref[rows, h * HEAD_DIM:(h + 1) * HEAD_DIM] for h in heads], axis=0)
        kh = band_of(k_ref, kp_ref, b, g)
        return lax.dot_general(kh, qs, (((1,), (1,)), ((), ())), preferred_element_type=F32)

    def finish(b, g, st):
        rows = slice(b * WINDOW, (b + 1) * WINDOW)
        heads = range(g * Q_PER_KV, (g + 1) * Q_PER_KV)
        sink = jnp.zeros((1, stacked), F32)
        for n, h in enumerate(heads):
            sink = jnp.where(lane_head == n, sink_ref[h] * LOG2E, sink)
        st = st + bias_ref[0 if b == 0 else 1]
        m = jnp.maximum(jnp.max(st, axis=0, keepdims=True), sink)
        p = jnp.exp2(st - m)
        denom = jnp.sum(p, axis=0, keepdims=True) + jnp.exp2(sink - m)
        vh = band_of(v_ref, vp_ref, b, g)
        ot = lax.dot_general(vh, p.astype(BF16), (((0,), (0,)), ((), ())), preferred_element_type=F32)
        ot = ot * (1.0 / denom)
        for n, h in enumerate(heads):
            o_ref[h * HEAD_DIM:(h + 1) * HEAD_DIM, rows] = ot[:, n * WINDOW:(n + 1) * WINDOW].astype(o_ref.dtype)

    pairs = [(b, g) for b in range(nblk) for g in range(N_KV_HEADS)]
    ahead, slots = 2, 3
    for n in range(ahead):
        st_ref[n] = scores(*pairs[n])
    for n, (b, g) in enumerate(pairs):
        if n + ahead < len(pairs):
            st_ref[(n + ahead) % slots] = scores(*pairs[n + ahead])
        finish(b, g, st_ref[n % slots])


def _swa(proj3, kv3, sinks, *, tq):
    b, t, _ = proj3.shape
    nblk = tq // WINDOW
    qcol = COL_Q // ATTN_WIDTH
    prev = lambda bi, i: jnp.maximum(i * nblk - 1, 0)
    return pl.pallas_call(
        functools.partial(_swa_kernel, nblk=nblk),
        grid=(b, t // tq),
        in_specs=[
            pl.BlockSpec(memory_space=pltpu.SMEM),
            pl.BlockSpec((None, tq, ATTN_WIDTH), lambda bi, i: (bi, i, qcol)),
            pl.BlockSpec((None, tq, KV_WIDTH), lambda bi, i: (bi, i, 0)),
            pl.BlockSpec((None, tq, KV_WIDTH), lambda bi, i: (bi, i, 1)),
            pl.BlockSpec((None, WINDOW, KV_WIDTH), lambda bi, i: (bi, prev(bi, i), 0)),
            pl.BlockSpec((None, WINDOW, KV_WIDTH), lambda bi, i: (bi, prev(bi, i), 1)),
        ],
        out_specs=pl.BlockSpec((None, ATTN_WIDTH, tq), lambda bi, i: (bi, 0, i)),
        out_shape=jax.ShapeDtypeStruct((b, ATTN_WIDTH, t), BF16),
        scratch_shapes=[pltpu.VMEM((2, 2 * WINDOW, Q_PER_KV * WINDOW), F32),
                        pltpu.VMEM((3, 2 * WINDOW, Q_PER_KV * WINDOW), F32)],
        compiler_params=_params(("parallel", "arbitrary")),
        name="swa",
    )(sinks, proj3, kv3, kv3, kv3, kv3)


def _mix_kernel(attn_ref, z_ref, gb_ref, gc_ref, zp_ref, gcp_ref, ga_ref, gg_ref, x_ref, cw_ref,
                wap_ref, wcp_ref, wmo_ref, o_ref, ext_ref, conv_ref, ya_ref, merged_ref,
                *, tm, tiles_per_seq, chunk):
    i = pl.program_id(0)
    cz = z_ref[...].astype(F32) * gc_ref[...].astype(F32)
    czp = zp_ref[...].astype(F32) * gcp_ref[...].astype(F32)
    czp = jnp.where(i % tiles_per_seq == 0, 0.0, czp)
    ext_ref[0:HALO, :] = czp
    ext_ref[HALO:, :] = cz
    cw = cw_ref[...]
    y = (cw[0:1, :] * ext_ref[HALO - 2:HALO - 2 + tm, :]
         + cw[1:2, :] * ext_ref[HALO - 1:HALO - 1 + tm, :]
         + cw[2:3, :] * cz)
    conv_ref[...] = (gb_ref[...].astype(F32) * y).astype(BF16)
    for c in range(o_ref.shape[1] // chunk):
        cols = slice(c * chunk, (c + 1) * chunk)
        ya_ref[:, cols] = lax.dot_general(attn_ref[...], wap_ref[:, cols], (((0,), (0,)), ((), ())),
                                          preferred_element_type=F32)
        y_attn = ya_ref[:, cols]
        y_conv = jnp.dot(conv_ref[...], wcp_ref[:, cols], preferred_element_type=F32)
        t_attn = jnp.tanh(ga_ref[:, cols].astype(F32))
        t_conv = jnp.tanh(gg_ref[:, cols].astype(F32))
        merged_ref[:, cols] = ((y_attn + y_conv) + (t_attn * y_attn + t_conv * y_conv)).astype(BF16)
    o_ref[...] = x_ref[...] + jnp.dot(merged_ref[...], wmo_ref[...], preferred_element_type=F32)


def _mix(attn_t, proj, x2, conv_w, wap, wcp, wmo, seq, *, tm, chunk):
    n, d = x2.shape
    cwid = wcp.shape[0]
    tps = seq // tm
    zc, gbc, gcc = COL_Z // cwid, COL_GB // cwid, COL_GC // cwid
    gac, ggc = COL_GATE_A // d, COL_GATE_C // d
    rb = tm // HALO
    const = lambda i: (0, 0)
    return pl.pallas_call(
        functools.partial(_mix_kernel, tm=tm, tiles_per_seq=tps, chunk=chunk),
        grid=(n // tm,),
        in_specs=[
            pl.BlockSpec((None, ATTN_WIDTH, tm), lambda i: (i // tps, 0, i % tps)),
            pl.BlockSpec((tm, cwid), lambda i: (i, zc)),
            pl.BlockSpec((tm, cwid), lambda i: (i, gbc)),
            pl.BlockSpec((tm, cwid), lambda i: (i, gcc)),
            pl.BlockSpec((HALO, cwid), lambda i: (jnp.maximum(i * rb - 1, 0), zc)),
            pl.BlockSpec((HALO, cwid), lambda i: (jnp.maximum(i * rb - 1, 0), gcc)),
            pl.BlockSpec((tm, d), lambda i: (i, gac)),
            pl.BlockSpec((tm, d), lambda i: (i, ggc)),
            pl.BlockSpec((tm, d), lambda i: (i, 0)),
            pl.BlockSpec((CONV_K, cwid), const),
            pl.BlockSpec(wap.shape, const),
            pl.BlockSpec(wcp.shape, const),
            pl.BlockSpec(wmo.shape, const),
        ],
        out_specs=pl.BlockSpec((tm, d), lambda i: (i, 0)),
        out_shape=jax.ShapeDtypeStruct((n, d), F32),
        scratch_shapes=[pltpu.VMEM((tm + HALO, cwid), F32), pltpu.VMEM((tm, cwid), BF16),
                        pltpu.VMEM((tm, d), F32), pltpu.VMEM((tm, d), BF16)],
        compiler_params=_params(("parallel",)),
        name="mix",
    )(attn_t, proj, proj, proj, proj, proj, proj, proj, x2, conv_w, wap, wcp, wmo)


def _memkv_kernel(m_ref, g_ref, w_ref, o_ref):
    u = _rms(m_ref[...], g_ref[...]).astype(BF16)
    o_ref[...] = jnp.dot(u, w_ref[...], preferred_element_type=F32).astype(o_ref.dtype)


def _memkv(mem2, g, w, *, tm):
    n, d = mem2.shape
    return pl.pallas_call(
        _memkv_kernel,
        grid=(n // tm,),
        in_specs=[
            pl.BlockSpec((tm, d), lambda i: (i, 0)),
            pl.BlockSpec((1, d), lambda i: (0, 0)),
            pl.BlockSpec(w.shape, lambda i: (0, 0)),
        ],
        out_specs=pl.BlockSpec((tm, w.shape[1]), lambda i: (i, 0)),
        out_shape=jax.ShapeDtypeStruct((n, w.shape[1]), BF16),
        compiler_params=_params(("parallel",)),
        name="memkv",
    )(mem2, g, w)


def _xattn_kernel(h_ref, g_ref, wq_ref, kv_ref, wo_ref, o_ref):
    h = h_ref[...]
    u = _rms(h, g_ref[...]).astype(BF16)
    q = jnp.dot(u, wq_ref[...], preferred_element_type=F32) * (X_HEAD_DIM ** -0.5)
    q = q.astype(BF16)
    xw = X_HEADS * X_HEAD_DIM
    outs = []
    for hd in range(X_HEADS):
        cols = slice(hd * X_HEAD_DIM, (hd + 1) * X_HEAD_DIM)
        kh = kv_ref[:, cols]
        vh = kv_ref[:, xw + hd * X_HEAD_DIM:xw + (hd + 1) * X_HEAD_DIM]
        s = lax.dot_general(q[:, cols], kh, (((1,), (1,)), ((), ())), preferred_element_type=F32)
        m = jnp.max(s, axis=-1, keepdims=True)
        p = jnp.exp(s - m)
        denom = jnp.sum(p, axis=-1, keepdims=True)
        outs.append((jnp.dot(p.astype(BF16), vh, preferred_element_type=F32) / denom).astype(BF16))
    o = jnp.concatenate(outs, axis=1)
    o_ref[...] = h + jnp.dot(o, wo_ref[...], preferred_element_type=F32)


def _xattn(h3, g, wq, kv3, wo, *, tm):
    b, t, d = h3.shape
    m, kvw = kv3.shape[1], kv3.shape[2]
    return pl.pallas_call(
        _xattn_kernel,
        grid=(b, t // tm),
        in_specs=[
            pl.BlockSpec((None, tm, d), lambda bi, i: (bi, i, 0)),
            pl.BlockSpec((1, d), lambda bi, i: (0, 0)),
            pl.BlockSpec(wq.shape, lambda bi, i: (0, 0)),
            pl.BlockSpec((None, m, kvw), lambda bi, i: (bi, 0, 0)),
            pl.BlockSpec(wo.shape, lambda bi, i: (0, 0)),
        ],
        out_specs=pl.BlockSpec((None, tm, d), lambda bi, i: (bi, i, 0)),
        out_shape=jax.ShapeDtypeStruct((b, t, d), F32),
        compiler_params=_params(("parallel", "parallel")),
        name="xattn",
    )(h3, g, wq, kv3, wo)


def _ffn_kernel(h_ref, g_ref, wa_ref, wb_ref, wo_ref, gf_ref, o_ref, u_ref, *, chunk):
    k = pl.program_id(1)

    @pl.when(k == 0)
    def _():
        h = h_ref[...]
        u_ref[...] = _rms(h, g_ref[...]).astype(BF16)
        o_ref[...] = h

    u = u_ref[...]
    a = jnp.dot(u, wa_ref[...], preferred_element_type=F32)
    b = jnp.dot(u, wb_ref[...], preferred_element_type=F32)
    act = (a * (1.0 + jnp.tanh(a)) * b).astype(BF16)
    for c in range(o_ref.shape[1] // chunk):
        cols = slice(c * chunk, (c + 1) * chunk)
        o_ref[:, cols] += jnp.dot(act, wo_ref[:, cols], preferred_element_type=F32)

    @pl.when(k == pl.num_programs(1) - 1)
    def _():
        o_ref[...] = _rms(o_ref[...], gf_ref[...])


def _ffn(h2, g, w_in, w_out, g_final, *, tm, th, chunk):
    n, d = h2.shape
    hidden = w_out.shape[0]
    nk = hidden // th
    return pl.pallas_call(
        functools.partial(_ffn_kernel, chunk=chunk),
        grid=(n // tm, nk),
        in_specs=[
            pl.BlockSpec((tm, d), lambda i, k: (i, 0)),
            pl.BlockSpec((1, d), lambda i, k: (0, 0)),
            pl.BlockSpec((d, th), lambda i, k: (0, k)),
            pl.BlockSpec((d, th), lambda i, k: (0, nk + k)),
            pl.BlockSpec((th, d), lambda i, k: (k, 0)),
            pl.BlockSpec((1, d), lambda i, k: (0, 0)),
        ],
        out_specs=pl.BlockSpec((tm, d), lambda i, k: (i, 0)),
        out_shape=jax.ShapeDtypeStruct((n, d), F32),
        scratch_shapes=[pltpu.VMEM((tm, d), BF16)],
        compiler_params=_params(("parallel", "arbitrary")),
        name="ffn",
    )(h2, g, w_in, w_in, w_out, g_final)


def _rope_tables(seq):
    half = HEAD_DIM // 2
    inv_freq = ROPE_THETA ** (-jnp.arange(half, dtype=F32) / half)
    ang = jnp.arange(seq, dtype=jnp.int32).astype(F32)[:, None] * inv_freq[None, :]
    cos, sin = jnp.cos(ang), jnp.sin(ang)
    reps = LANES // HEAD_DIM
    cos_t = jnp.tile(jnp.concatenate([cos, cos], axis=1), (1, reps))
    sin_t = jnp.tile(jnp.concatenate([-sin, sin], axis=1), (1, reps))
    return cos_t, sin_t


def _relayout_w_in(w):
    sizes = (ATTN_WIDTH, KV_WIDTH, KV_WIDTH, CONV_WIDTH, CONV_WIDTH, CONV_WIDTH, D_MODEL, D_MODEL)
    q, k, v, z, gb, gc, ga, gg = jnp.split(w, np.cumsum(sizes)[:-1].tolist(), axis=1)
    main = jnp.concatenate([0.5 * ga, 0.5 * gg, z, gb, gc, q], axis=1)
    return main.astype(BF16), jnp.concatenate([k, v], axis=1).astype(BF16)


def kernel(x, mem, g_mix, w_in, conv_w, attn_sinks, w_attn_proj, w_conv_proj, w_mix_out, g_xattn, g_mem,
           w_xq, w_xkv, w_xo, g_ffn, w_ffn_in, w_ffn_out, g_final):
    b, t, d = x.shape
    depth = w_in.shape[0]
    cos_t, sin_t = _rope_tables(t)
    h = x.reshape(b * t, d)
    mem2 = mem.reshape(b * mem.shape[1], d)
    for l in range(depth):
        w_main, w_kv = _relayout_w_in(w_in[l])
        u, kv = _norm_kv(h, g_mix[l][None, :], w_kv, cos_t, sin_t, t, tm=512)
        q_scale = HEAD_DIM ** -0.5 * LOG2E
        proj = _inproj(u, w_main, cos_t * q_scale, sin_t * q_scale, t, tm=512, tn=2048, chunk=512)
        attn = _swa(proj.reshape(b, t, -1), kv.reshape(b, t, -1), attn_sinks[l], tq=512)
        h = _mix(attn, proj, h, conv_w[l], w_attn_proj[l].astype(BF16),
                 w_conv_proj[l].astype(BF16), (0.5 * w_mix_out[l]).astype(BF16), t, tm=256, chunk=1024)
        mkv = _memkv(mem2, g_mem[l][None, :], w_xkv[l].astype(BF16), tm=256)
        h = _xattn(h.reshape(b, t, d), g_xattn[l][None, :], w_xq[l].astype(BF16),
                   mkv.reshape(b, mem.shape[1], -1), w_xo[l].astype(BF16), tm=512).reshape(b * t, d)
        last = l == depth - 1
        assert last, "fused final norm expects a single layer"
        hidden = w_ffn_out.shape[1]
        col_scale = jnp.where(jnp.arange(2 * hidden) < hidden, 0.5, 1.0).astype(F32)
        h = _ffn(h, g_ffn[l][None, :], (w_ffn_in[l] * col_scale).astype(BF16), w_ffn_out[l].astype(BF16),
                 g_final[None, :], tm=512, th=512, chunk=2048)
    return h.reshape(b, t, d)
```

```python
import functools

import jax
import jax.numpy as jnp
import numpy as np
from jax import lax
from jax.experimental import pallas as pl
from jax.experimental.pallas import tpu as pltpu

F32 = jnp.float32
BF16 = jnp.bfloat16

HEAD_DIM = 64
N_Q_HEADS = 16
N_KV_HEADS = 4
Q_PER_KV = N_Q_HEADS // N_KV_HEADS
ATTN_WIDTH = N_Q_HEADS * HEAD_DIM
KV_WIDTH = N_KV_HEADS * HEAD_DIM
WINDOW = 128
ROPE_THETA = 10000.0
CONV_K = 3
X_HEADS = 4
X_HEAD_DIM = 128
EPS = 1e-6
LOG2E = 1.4426950408889634

D_MODEL = 2048
CONV_WIDTH = 1024
REF_Q = 0
REF_K = REF_Q + ATTN_WIDTH
REF_V = REF_K + KV_WIDTH
REF_Z = REF_V + KV_WIDTH
REF_GB = REF_Z + CONV_WIDTH
REF_GC = REF_GB + CONV_WIDTH
REF_GATE_A = REF_GC + CONV_WIDTH
REF_GATE_C = REF_GATE_A + D_MODEL
COL_GATE_A = 0
COL_GATE_C = COL_GATE_A + D_MODEL
COL_Z = COL_GATE_C + D_MODEL
COL_GB = COL_Z + CONV_WIDTH
COL_GC = COL_GB + CONV_WIDTH
COL_Q = COL_GC + CONV_WIDTH
MAIN_WIDTH = COL_Q + ATTN_WIDTH
_MAIN_SRC = ((REF_GATE_A, D_MODEL), (REF_GATE_C, D_MODEL), (REF_Z, CONV_WIDTH), (REF_GB, CONV_WIDTH),
             (REF_GC, CONV_WIDTH), (REF_Q, ATTN_WIDTH))

LANES = 128
HALO = 16
VMEM_LIMIT = 56 * 1024 * 1024


def _rms(x, g):
    ms = jnp.mean(x * x, axis=-1, keepdims=True)
    return x * lax.rsqrt(ms + EPS) * g


def _params(sem):
    return pltpu.CompilerParams(dimension_semantics=sem, vmem_limit_bytes=VMEM_LIMIT)


def _rope(acc, cos, sin_signed):
    lane = lax.broadcasted_iota(jnp.int32, (1, LANES), 1)
    first_half = (lane % HEAD_DIM) < (HEAD_DIM // 2)
    outs = []
    for c in range(acc.shape[1] // LANES):
        xc = acc[:, c * LANES:(c + 1) * LANES]
        fwd = pltpu.roll(xc, LANES - HEAD_DIM // 2, axis=1)
        bwd = pltpu.roll(xc, HEAD_DIM // 2, axis=1)
        outs.append(xc * cos + jnp.where(first_half, fwd, bwd) * sin_signed)
    return jnp.concatenate(outs, axis=1)


def _norm_kv_kernel(x_ref, g_ref, w_ref, cos_ref, sin_ref, u_ref, kv_ref):
    u = _rms(x_ref[...], g_ref[...]).astype(BF16)
    u_ref[...] = u
    k = jnp.dot(u, w_ref[:, :KV_WIDTH], preferred_element_type=F32)
    kv_ref[:, :KV_WIDTH] = _rope(k, cos_ref[...], sin_ref[...]).astype(kv_ref.dtype)
    kv_ref[:, KV_WIDTH:] = jnp.dot(u, w_ref[:, KV_WIDTH:], preferred_element_type=F32).astype(kv_ref.dtype)


def _norm_kv(x2, g, w_in, cos_t, sin_t, seq, *, tm):
    n, d = x2.shape
    tpb = seq // tm
    assert REF_V == REF_K + KV_WIDTH and REF_K % (2 * KV_WIDTH) == 0
    return pl.pallas_call(
        _norm_kv_kernel,
        grid=(n // tm,),
        in_specs=[
            pl.BlockSpec((tm, d), lambda i: (i, 0)),
            pl.BlockSpec((1, d), lambda i: (0, 0)),
            pl.BlockSpec((d, 2 * KV_WIDTH), lambda i: (0, REF_K // (2 * KV_WIDTH))),
            pl.BlockSpec((tm, LANES), lambda i: (i % tpb, 0)),
            pl.BlockSpec((tm, LANES), lambda i: (i % tpb, 0)),
        ],
        out_specs=[
            pl.BlockSpec((tm, d), lambda i: (i, 0)),
            pl.BlockSpec((tm, 2 * KV_WIDTH), lambda i: (i, 0)),
        ],
        out_shape=[
            jax.ShapeDtypeStruct((n, d), BF16),
            jax.ShapeDtypeStruct((n, 2 * KV_WIDTH), BF16),
        ],
        compiler_params=_params(("parallel",)),
        name="norm_kv",
    )(x2, g, w_in, cos_t, sin_t)


def _inproj_kernel(u_ref, *refs, tn, chunk):
    n_chunks = tn // chunk
    w_refs, (cos_ref, sin_ref, o_ref) = refs[:n_chunks], refs[n_chunks:]
    j = pl.program_id(0)
    n_tiles = MAIN_WIDTH // tn
    q_tile, q_lo = COL_Q // tn, COL_Q % tn
    assert q_lo % chunk == 0 and MAIN_WIDTH == (q_tile + 1) * tn

    def tile(rope_from):
        for c in range(n_chunks):
            acc = jnp.dot(u_ref[...], w_refs[c][...], preferred_element_type=F32)
            if c * chunk >= rope_from:
                acc = _rope(acc, cos_ref[...], sin_ref[...])
            o_ref[:, c * chunk:(c + 1) * chunk] = acc.astype(o_ref.dtype)

    pl.when(j < n_tiles - 1)(lambda: tile(tn))
    pl.when(j == n_tiles - 1)(lambda: tile(q_lo))


def _inproj(u, w_in, cos_t, sin_t, seq, *, tm, tn, chunk):
    n, d = u.shape
    tpb = seq // tm
    n_tiles, n_chunks = MAIN_WIDTH // tn, tn // chunk
    src_blocks = []
    for ref_off, width in _MAIN_SRC:
        assert ref_off % chunk == 0 and width % chunk == 0
        src_blocks += [ref_off // chunk + b for b in range(width // chunk)]
    assert len(src_blocks) == n_tiles * n_chunks

    def w_spec(c):
        def index(j, i):
            blk = 0
            for jj in range(n_tiles):
                blk = blk + jnp.where(j == jj, src_blocks[jj * n_chunks + c], 0)
            return (0, blk)
        return pl.BlockSpec((d, chunk), index)

    return pl.pallas_call(
        functools.partial(_inproj_kernel, tn=tn, chunk=chunk),
        grid=(n_tiles, n // tm),
        in_specs=[pl.BlockSpec((tm, d), lambda j, i: (i, 0))]
        + [w_spec(c) for c in range(n_chunks)]
        + [pl.BlockSpec((tm, LANES), lambda j, i: (i % tpb, 0)),
           pl.BlockSpec((tm, LANES), lambda j, i: (i % tpb, 0))],
        out_specs=pl.BlockSpec((tm, tn), lambda j, i: (i, j)),
        out_shape=jax.ShapeDtypeStruct((n, MAIN_WIDTH), BF16),
        compiler_params=_params(("parallel", "parallel")),
        name="inproj",
    )(u, *([w_in] * n_chunks), cos_t, sin_t)


def _swa_kernel(sink_ref, q_ref, k_ref, v_ref, kp_ref, vp_ref, o_ref, bias_ref, st_ref, *, nblk):
    i = pl.program_id(1)
    stacked = Q_PER_KV * WINDOW
    key = lax.broadcasted_iota(jnp.int32, (2 * WINDOW, stacked), 0)
    qry = lax.broadcasted_iota(jnp.int32, (2 * WINDOW, stacked), 1) % WINDOW
    band = (key > qry) & (key <= qry + WINDOW)
    bias_ref[0] = jnp.where(band & ((key >= WINDOW) | (i > 0)), 0.0, -jnp.inf)
    bias_ref[1] = jnp.where(band, 0.0, -jnp.inf)
    lane_head = lax.broadcasted_iota(jnp.int32, (1, stacked), 1) // WINDOW

    def band_of(ref, prev_ref, b, g):
        cols = slice(g * HEAD_DIM, (g + 1) * HEAD_DIM)
        if b == 0:
            return jnp.concatenate([prev_ref[:, cols], ref[0:WINDOW, cols]], axis=0)
        return ref[(b - 1) * WINDOW:(b + 1) * WINDOW, cols]

    def scores(b, g):
        rows = slice(b * WINDOW, (b + 1) * WINDOW)
        heads = range(g * Q_PER_KV, (g + 1) * Q_PER_KV)
        qs = jnp.concatenate([q_ref[rows, h * HEAD_DIM:(h + 1) * HEAD_DIM] for h in heads], axis=0)
        kh = band_of(k_ref, kp_ref, b, g)
        return lax.dot_general(kh, qs, (((1,), (1,)), ((), ())), preferred_element_type=F32)

    def finish(b, g, st):
        rows = slice(b * WINDOW, (b + 1) * WINDOW)
        heads = range(g * Q_PER_KV, (g + 1) * Q_PER_KV)
        sink = jnp.zeros((1, stacked), F32)
        for n, h in enumerate(heads):
            sink = jnp.where(lane_head == n, sink_ref[h] * LOG2E, sink)
        st = st + bias_ref[0 if b == 0 else 1]
        m = jnp.maximum(jnp.max(st, axis=0, keepdims=True), sink)
        p = jnp.exp2(st - m)
        denom = jnp.sum(p, axis=0, keepdims=True) + jnp.exp2(sink - m)
        vh = band_of(v_ref, vp_ref, b, g)
        ot = lax.dot_general(vh, p.astype(BF16), (((0,), (0,)), ((), ())), preferred_element_type=F32)
        ot = ot * (1.0 / denom)
        for n, h in enumerate(heads):
            o_ref[h * HEAD_DIM:(h + 1) * HEAD_DIM, rows] = ot[:, n * WINDOW:(n + 1) * WINDOW].astype(o_ref.dtype)

    pairs = [(b, g) for b in range(nblk) for g in range(N_KV_HEADS)]
    ahead, slots = 2, 3
    for n in range(ahead):
        st_ref[n] = scores(*pairs[n])
    for n, (b, g) in enumerate(pairs):
        if n + ahead < len(pairs):
            st_ref[(n + ahead) % slots] = scores(*pairs[n + ahead])
        finish(b, g, st_ref[n % slots])


def _swa(proj3, kv3, sinks, *, tq):
    b, t, _ = proj3.shape
    nblk = tq // WINDOW
    qcol = COL_Q // ATTN_WIDTH
    prev = lambda bi, i: jnp.maximum(i * nblk - 1, 0)
    return pl.pallas_call(
        functools.partial(_swa_kernel, nblk=nblk),
        grid=(b, t // tq),
        in_specs=[
            pl.BlockSpec(memory_space=pltpu.SMEM),
            pl.BlockSpec((None, tq, ATTN_WIDTH), lambda bi, i: (bi, i, qcol)),
            pl.BlockSpec((None, tq, KV_WIDTH), lambda bi, i: (bi, i, 0)),
            pl.BlockSpec((None, tq, KV_WIDTH), lambda bi, i: (bi, i, 1)),
            pl.BlockSpec((None, WINDOW, KV_WIDTH), lambda bi, i: (bi, prev(bi, i), 0)),
            pl.BlockSpec((None, WINDOW, KV_WIDTH), lambda bi, i: (bi, prev(bi, i), 1)),
        ],
        out_specs=pl.BlockSpec((None, ATTN_WIDTH, tq), lambda bi, i: (bi, 0, i)),
        out_shape=jax.ShapeDtypeStruct((b, ATTN_WIDTH, t), BF16),
        scratch_shapes=[pltpu.VMEM((2, 2 * WINDOW, Q_PER_KV * WINDOW), F32),
                        pltpu.VMEM((3, 2 * WINDOW, Q_PER_KV * WINDOW), F32)],
        compiler_params=_params(("parallel", "arbitrary")),
        name="swa",
    )(sinks, proj3, kv3, kv3, kv3, kv3)


def _mix_kernel(attn_ref, z_ref, gb_ref, gc_ref, zp_ref, gcp_ref, ga_ref, gg_ref, x_ref, cw_ref,
                wap_ref, wcp_ref, wmo_ref, o_ref, ext_ref, conv_ref, ya_ref, merged_ref,
                *, tm, tiles_per_seq, chunk):
    i = pl.program_id(0)
    cz = z_ref[...].astype(F32) * gc_ref[...].astype(F32)
    czp = zp_ref[...].astype(F32) * gcp_ref[...].astype(F32)
    czp = jnp.where(i % tiles_per_seq == 0, 0.0, czp)
    ext_ref[0:HALO, :] = czp
    ext_ref[HALO:, :] = cz
    cw = cw_ref[...]
    y = (cw[0:1, :] * ext_ref[HALO - 2:HALO - 2 + tm, :]
         + cw[1:2, :] * ext_ref[HALO - 1:HALO - 1 + tm, :]
         + cw[2:3, :] * cz)
    conv_ref[...] = (gb_ref[...].astype(F32) * y).astype(BF16)
    for c in range(o_ref.shape[1] // chunk):
        cols = slice(c * chunk, (c + 1) * chunk)
        ya_ref[:, cols] = lax.dot_general(attn_ref[...], wap_ref[:, cols], (((0,), (0,)), ((), ())),
                                          preferred_element_type=F32)
        y_attn = ya_ref[:, cols]
        y_conv = jnp.dot(conv_ref[...], wcp_ref[:, cols], preferred_element_type=F32)
        t_attn = jnp.tanh(ga_ref[:, cols].astype(F32))
        t_conv = jnp.tanh(gg_ref[:, cols].astype(F32))
        merged_ref[:, cols] = ((y_attn + y_conv) + (t_attn * y_attn + t_conv * y_conv)).astype(BF16)
    o_ref[...] = x_ref[...] + jnp.dot(merged_ref[...], wmo_ref[...], preferred_element_type=F32)


def _mix(attn_t, proj, x2, conv_w, wap, wcp, wmo, seq, *, tm, chunk):
    n, d = x2.shape
    cwid = wcp.shape[0]
    tps = seq // tm
    zc, gbc, gcc = COL_Z // cwid, COL_GB // cwid, COL_GC // cwid
    gac, ggc = COL_GATE_A // d, COL_GATE_C // d
    rb = tm // HALO
    const = lambda i: (0, 0)
    return pl.pallas_call(
        functools.partial(_mix_kernel, tm=tm, tiles_per_seq=tps, chunk=chunk),
        grid=(n // tm,),
        in_specs=[
            pl.BlockSpec((None, ATTN_WIDTH, tm), lambda i: (i // tps, 0, i % tps)),
            pl.BlockSpec((tm, cwid), lambda i: (i, zc)),
            pl.BlockSpec((tm, cwid), lambda i: (i, gbc)),
            pl.BlockSpec((tm, cwid), lambda i: (i, gcc)),
            pl.BlockSpec((HALO, cwid), lambda i: (jnp.maximum(i * rb - 1, 0), zc)),
            pl.BlockSpec((HALO, cwid), lambda i: (jnp.maximum(i * rb - 1, 0), gcc)),
            pl.BlockSpec((tm, d), lambda i: (i, gac)),
            pl.BlockSpec((tm, d), lambda i: (i, ggc)),
            pl.BlockSpec((tm, d), lambda i: (i, 0)),
            pl.BlockSpec((CONV_K, cwid), const),
            pl.BlockSpec(wap.shape, const),
            pl.BlockSpec(wcp.shape, const),
            pl.BlockSpec(wmo.shape, const),
        ],
        out_specs=pl.BlockSpec((tm, d), lambda i: (i, 0)),
        out_shape=jax.ShapeDtypeStruct((n, d), F32),
        scratch_shapes=[pltpu.VMEM((tm + HALO, cwid), F32), pltpu.VMEM((tm, cwid), BF16),
                        pltpu.VMEM((tm, d), F32), pltpu.VMEM((tm, d), BF16)],
        compiler_params=_params(("parallel",)),
        name="mix",
    )(attn_t, proj, proj, proj, proj, proj, proj, proj, x2, conv_w, wap, wcp, wmo)


def _memkv_kernel(m_ref, g_ref, w_ref, o_ref):
    u = _rms(m_ref[...], g_ref[...]).astype(BF16)
    o_ref[...] = jnp.dot(u, w_ref[...], preferred_element_type=F32).astype(o_ref.dtype)


def _memkv(mem2, g, w, *, tm):
    n, d = mem2.shape
    return pl.pallas_call(
        _memkv_kernel,
        grid=(n // tm,),
        in_specs=[
            pl.BlockSpec((tm, d), lambda i: (i, 0)),
            pl.BlockSpec((1, d), lambda i: (0, 0)),
            pl.BlockSpec(w.shape, lambda i: (0, 0)),
        ],
        out_specs=pl.BlockSpec((tm, w.shape[1]), lambda i: (i, 0)),
        out_shape=jax.ShapeDtypeStruct((n, w.shape[1]), BF16),
        compiler_params=_params(("parallel",)),
        name="memkv",
    )(mem2, g, w)


def _xattn_kernel(h_ref, g_ref, wq_ref, kv_ref, wo_ref, o_ref):
    h = h_ref[...]
    u = _rms(h, g_ref[...]).astype(BF16)
    q = jnp.dot(u, wq_ref[...], preferred_element_type=F32) * (X_HEAD_DIM ** -0.5)
    q = q.astype(BF16)
    xw = X_HEADS * X_HEAD_DIM
    outs = []
    for hd in range(X_HEADS):
        cols = slice(hd * X_HEAD_DIM, (hd + 1) * X_HEAD_DIM)
        kh = kv_ref[:, cols]
        vh = kv_ref[:, xw + hd * X_HEAD_DIM:xw + (hd + 1) * X_HEAD_DIM]
        s = lax.dot_general(q[:, cols], kh, (((1,), (1,)), ((), ())), preferred_element_type=F32)
        m = jnp.max(s, axis=-1, keepdims=True)
        p = jnp.exp(s - m)
        denom = jnp.sum(p, axis=-1, keepdims=True)
        outs.append((jnp.dot(p.astype(BF16), vh, preferred_element_type=F32) / denom).astype(BF16))
    o = jnp.concatenate(outs, axis=1)
    o_ref[...] = h + jnp.dot(o, wo_ref[...], preferred_element_type=F32)


def _xattn(h3, g, wq, kv3, wo, *, tm):
    b, t, d = h3.shape
    m, kvw = kv3.shape[1], kv3.shape[2]
    return pl.pallas_call(
        _xattn_kernel,
        grid=(b, t // tm),
        in_specs=[
            pl.BlockSpec((None, tm, d), lambda bi, i: (bi, i, 0)),
            pl.BlockSpec((1, d), lambda bi, i: (0, 0)),
            pl.BlockSpec(wq.shape, lambda bi, i: (0, 0)),
            pl.BlockSpec((None, m, kvw), lambda bi, i: (bi, 0, 0)),
            pl.BlockSpec(wo.shape, lambda bi, i: (0, 0)),
        ],
        out_specs=pl.BlockSpec((None, tm, d), lambda bi, i: (bi, i, 0)),
        out_shape=jax.ShapeDtypeStruct((b, t, d), F32),
        compiler_params=_params(("parallel", "parallel")),
        name="xattn",
    )(h3, g, wq, kv3, wo)


def _ffn_kernel(h_ref, g_ref, wa_ref, wb_ref, wo_ref, gf_ref, o_ref, u_ref, *, chunk):
    k = pl.program_id(1)

    @pl.when(k == 0)
    def _():
        h = h_ref[...]
        u_ref[...] = _rms(h, g_ref[...]).astype(BF16)
        o_ref[...] = h

    u = u_ref[...]
    a = jnp.dot(u, wa_ref[...], preferred_element_type=F32)
    b = jnp.dot(u, wb_ref[...], preferred_element_type=F32)
    act = (a * (1.0 + jnp.tanh(a)) * b).astype(BF16)
    for c in range(o_ref.shape[1] // chunk):
        cols = slice(c * chunk, (c + 1) * chunk)
        o_ref[:, cols] += jnp.dot(act, wo_ref[:, cols], preferred_element_type=F32)

    @pl.when(k == pl.num_programs(1) - 1)
    def _():
        o_ref[...] = _rms(o_ref[...], gf_ref[...])


def _ffn(h2, g, w_in, w_out, g_final, *, tm, th, chunk):
    n, d = h2.shape
    hidden = w_out.shape[0]
    nk = hidden // th
    return pl.pallas_call(
        functools.partial(_ffn_kernel, chunk=chunk),
        grid=(n // tm, nk),
        in_specs=[
            pl.BlockSpec((tm, d), lambda i, k: (i, 0)),
            pl.BlockSpec((1, d), lambda i, k: (0, 0)),
            pl.BlockSpec((None, d, th), lambda i, k: (k, 0, 0)),
            pl.BlockSpec((None, d, th), lambda i, k: (nk + k, 0, 0)),
            pl.BlockSpec((th, d), lambda i, k: (k, 0)),
            pl.BlockSpec((1, d), lambda i, k: (0, 0)),
        ],
        out_specs=pl.BlockSpec((tm, d), lambda i, k: (i, 0)),
        out_shape=jax.ShapeDtypeStruct((n, d), F32),
        scratch_shapes=[pltpu.VMEM((tm, d), BF16)],
        compiler_params=_params(("parallel", "arbitrary")),
        name="ffn",
    )(h2, g, w_in, w_in, w_out, g_final)


def _rope_tables(seq):
    half = HEAD_DIM // 2
    inv_freq = ROPE_THETA ** (-jnp.arange(half, dtype=F32) / half)
    ang = jnp.arange(seq, dtype=jnp.int32).astype(F32)[:, None] * inv_freq[None, :]
    cos, sin = jnp.cos(ang), jnp.sin(ang)
    reps = LANES // HEAD_DIM
    cos_t = jnp.tile(jnp.concatenate([cos, cos], axis=1), (1, reps))
    sin_t = jnp.tile(jnp.concatenate([-sin, sin], axis=1), (1, reps))
    return cos_t, sin_t


def _cast_w_in(w):
    col_scale = jnp.where(jnp.arange(w.shape[1]) >= REF_GATE_A, 0.5, 1.0).astype(F32)
    return (w * col_scale).astype(BF16)


def _cast_w_ffn_in(w, th):
    d, two_hidden = w.shape
    col_scale = jnp.where(jnp.arange(two_hidden) < two_hidden // 2, 0.5, 1.0).astype(F32)
    tiles = (w * col_scale).astype(BF16).reshape(d, two_hidden // th, th)
    return jnp.transpose(tiles, (1, 0, 2))


def kernel(x, mem, g_mix, w_in, conv_w, attn_sinks, w_attn_proj, w_conv_proj, w_mix_out, g_xattn, g_mem,
           w_xq, w_xkv, w_xo, g_ffn, w_ffn_in, w_ffn_out, g_final):
    b, t, d = x.shape
    depth = w_in.shape[0]
    cos_t, sin_t = _rope_tables(t)
    h = x.reshape(b * t, d)
    mem2 = mem.reshape(b * mem.shape[1], d)
    for l in range(depth):
        w_in_bf = _cast_w_in(w_in[l])
        u, kv = _norm_kv(h, g_mix[l][None, :], w_in_bf, cos_t, sin_t, t, tm=512)
        q_scale = HEAD_DIM ** -0.5 * LOG2E
        proj = _inproj(u, w_in_bf, cos_t * q_scale, sin_t * q_scale, t, tm=1024, tn=2048, chunk=512)
        attn = _swa(proj.reshape(b, t, -1), kv.reshape(b, t, -1), attn_sinks[l], tq=512)
        h = _mix(attn, proj, h, conv_w[l], w_attn_proj[l].astype(BF16),
                 w_conv_proj[l].astype(BF16), (0.5 * w_mix_out[l]).astype(BF16), t, tm=256, chunk=1024)
        mkv = _memkv(mem2, g_mem[l][None, :], w_xkv[l].astype(BF16), tm=256)
        h = _xattn(h.reshape(b, t, d), g_xattn[l][None, :], w_xq[l].astype(BF16),
                   mkv.reshape(b, mem.shape[1], -1), w_xo[l].astype(BF16), tm=512).reshape(b * t, d)
        last = l == depth - 1
        assert last, "fused final norm expects a single layer"
        th = 512
        h = _ffn(h, g_ffn[l][None, :], _cast_w_ffn_in(w_ffn_in[l], th), w_ffn_out[l].astype(BF16),
                 g_final[None, :], tm=512, th=th, chunk=2048)
    return h.reshape(b, t, d)
```

```python
import functools

import jax
import jax.numpy as jnp
import numpy as np
from jax import lax
from jax.experimental import pallas as pl
from jax.experimental.pallas import tpu as pltpu

F32 = jnp.float32
BF16 = jnp.bfloat16

HEAD_DIM = 64
N_Q_HEADS = 16
N_KV_HEADS = 4
Q_PER_KV = N_Q_HEADS // N_KV_HEADS
ATTN_WIDTH = N_Q_HEADS * HEAD_DIM
KV_WIDTH = N_KV_HEADS * HEAD_DIM
WINDOW = 128
ROPE_THETA = 10000.0
CONV_K = 3
X_HEADS = 4
X_HEAD_DIM = 128
EPS = 1e-6
LOG2E = 1.4426950408889634

D_MODEL = 2048
CONV_WIDTH = 1024
REF_Q = 0
REF_K = REF_Q + ATTN_WIDTH
REF_V = REF_K + KV_WIDTH
REF_Z = REF_V + KV_WIDTH
REF_GB = REF_Z + CONV_WIDTH
REF_GC = REF_GB + CONV_WIDTH
REF_GATE_A = REF_GC + CONV_WIDTH
REF_GATE_C = REF_GATE_A + D_MODEL
COL_GATE_A = 0
COL_GATE_C = COL_GATE_A + D_MODEL
COL_Z = COL_GATE_C + D_MODEL
COL_GB = COL_Z + CONV_WIDTH
COL_GC = COL_GB + CONV_WIDTH
COL_Q = COL_GC + CONV_WIDTH
MAIN_WIDTH = COL_Q + ATTN_WIDTH
_MAIN_SRC = ((REF_GATE_A, D_MODEL), (REF_GATE_C, D_MODEL), (REF_Z, CONV_WIDTH), (REF_GB, CONV_WIDTH),
             (REF_GC, CONV_WIDTH), (REF_Q, ATTN_WIDTH))

LANES = 128
HALO = 16
VMEM_LIMIT = 56 * 1024 * 1024


def _rms(x, g):
    ms = jnp.mean(x * x, axis=-1, keepdims=True)
    return x * lax.rsqrt(ms + EPS) * g


def _params(sem):
    return pltpu.CompilerParams(dimension_semantics=sem, vmem_limit_bytes=VMEM_LIMIT)


def _rope(acc, cos, sin_signed):
    lane = lax.broadcasted_iota(jnp.int32, (1, LANES), 1)
    first_half = (lane % HEAD_DIM) < (HEAD_DIM // 2)
    outs = []
    for c in range(acc.shape[1] // LANES):
        xc = acc[:, c * LANES:(c + 1) * LANES]
        fwd = pltpu.roll(xc, LANES - HEAD_DIM // 2, axis=1)
        bwd = pltpu.roll(xc, HEAD_DIM // 2, axis=1)
        outs.append(xc * cos + jnp.where(first_half, fwd, bwd) * sin_signed)
    return jnp.concatenate(outs, axis=1)


def _norm_kv_kernel(x_ref, g_ref, w_ref, cos_ref, sin_ref, u_ref, kv_ref):
    u = _rms(x_ref[...], g_ref[...]).astype(BF16)
    u_ref[...] = u
    k = jnp.dot(u, w_ref[:, :KV_WIDTH], preferred_element_type=F32)
    kv_ref[:, :KV_WIDTH] = _rope(k, cos_ref[...], sin_ref[...]).astype(kv_ref.dtype)
    kv_ref[:, KV_WIDTH:] = jnp.dot(u, w_ref[:, KV_WIDTH:], preferred_element_type=F32).astype(kv_ref.dtype)


def _norm_kv(x2, g, w_in, cos_t, sin_t, seq, *, tm):
    n, d = x2.shape
    tpb = seq // tm
    assert REF_V == REF_K + KV_WIDTH and REF_K % (2 * KV_WIDTH) == 0
    return pl.pallas_call(
        _norm_kv_kernel,
        grid=(n // tm,),
        in_specs=[
            pl.BlockSpec((tm, d), lambda i: (i, 0)),
            pl.BlockSpec((1, d), lambda i: (0, 0)),
            pl.BlockSpec((d, 2 * KV_WIDTH), lambda i: (0, REF_K // (2 * KV_WIDTH))),
            pl.BlockSpec((tm, LANES), lambda i: (i % tpb, 0)),
            pl.BlockSpec((tm, LANES), lambda i: (i % tpb, 0)),
        ],
        out_specs=[
            pl.BlockSpec((tm, d), lambda i: (i, 0)),
            pl.BlockSpec((tm, 2 * KV_WIDTH), lambda i: (i, 0)),
        ],
        out_shape=[
            jax.ShapeDtypeStruct((n, d), BF16),
            jax.ShapeDtypeStruct((n, 2 * KV_WIDTH), BF16),
        ],
        compiler_params=_params(("parallel",)),
        name="norm_kv",
    )(x2, g, w_in, cos_t, sin_t)


def _inproj_kernel(u_ref, *refs, tn, chunk):
    n_chunks = tn // chunk
    w_refs, (cos_ref, sin_ref, o_ref) = refs[:n_chunks], refs[n_chunks:]
    j = pl.program_id(0)
    n_tiles = MAIN_WIDTH // tn
    q_tile, q_lo = COL_Q // tn, COL_Q % tn
    assert q_lo % chunk == 0 and MAIN_WIDTH == (q_tile + 1) * tn

    def tile(rope_from):
        for c in range(n_chunks):
            acc = jnp.dot(u_ref[...], w_refs[c][...], preferred_element_type=F32)
            if c * chunk >= rope_from:
                acc = _rope(acc, cos_ref[...], sin_ref[...])
            o_ref[:, c * chunk:(c + 1) * chunk] = acc.astype(o_ref.dtype)

    pl.when(j < n_tiles - 1)(lambda: tile(tn))
    pl.when(j == n_tiles - 1)(lambda: tile(q_lo))


def _inproj(u, w_in, cos_t, sin_t, seq, *, tm, tn, chunk):
    n, d = u.shape
    tpb = seq // tm
    n_tiles, n_chunks = MAIN_WIDTH // tn, tn // chunk
    src_blocks = []
    for ref_off, width in _MAIN_SRC:
        assert ref_off % chunk == 0 and width % chunk == 0
        src_blocks += [ref_off // chunk + b for b in range(width // chunk)]
    assert len(src_blocks) == n_tiles * n_chunks

    def w_spec(c):
        def index(j, i):
            blk = 0
            for jj in range(n_tiles):
                blk = blk + jnp.where(j == jj, src_blocks[jj * n_chunks + c], 0)
            return (0, blk)
        return pl.BlockSpec((d, chunk), index)

    return pl.pallas_call(
        functools.partial(_inproj_kernel, tn=tn, chunk=chunk),
        grid=(n_tiles, n // tm),
        in_specs=[pl.BlockSpec((tm, d), lambda j, i: (i, 0))]
        + [w_spec(c) for c in range(n_chunks)]
        + [pl.BlockSpec((tm, LANES), lambda j, i: (i % tpb, 0)),
           pl.BlockSpec((tm, LANES), lambda j, i: (i % tpb, 0))],
        out_specs=pl.BlockSpec((tm, tn), lambda j, i: (i, j)),
        out_shape=jax.ShapeDtypeStruct((n, MAIN_WIDTH), BF16),
        compiler_params=_params(("parallel", "parallel")),
        name="inproj",
    )(u, *([w_in] * n_chunks), cos_t, sin_t)


def _swa_kernel(sink_ref, q_ref, k_ref, v_ref, kp_ref, vp_ref, o_ref, bias_ref, st_ref, *, nblk):
    i = pl.program_id(1)
    stacked = Q_PER_KV * WINDOW
    key = lax.broadcasted_iota(jnp.int32, (2 * WINDOW, stacked), 0)
    qry = lax.broadcasted_iota(jnp.int32, (2 * WINDOW, stacked), 1) % WINDOW
    band = (key > qry) & (key <= qry + WINDOW)
    bias_ref[0] = jnp.where(band & ((key >= WINDOW) | (i > 0)), 0.0, -jnp.inf)
    bias_ref[1] = jnp.where(band, 0.0, -jnp.inf)
    lane_head = lax.broadcasted_iota(jnp.int32, (1, stacked), 1) // WINDOW

    def band_of(ref, prev_ref, b, g):
        cols = slice(g * HEAD_DIM, (g + 1) * HEAD_DIM)
        if b == 0:
            return jnp.concatenate([prev_ref[:, cols], ref[0:WINDOW, cols]], axis=0)
        return ref[(b - 1) * WINDOW:(b + 1) * WINDOW, cols]

    def scores(b, g):
        rows = slice(b * WINDOW, (b + 1) * WINDOW)
        heads = range(g * Q_PER_KV, (g + 1) * Q_PER_KV)
        qs = jnp.concatenate([q_ref[rows, h * HEAD_DIM:(h + 1) * HEAD_DIM] for h in heads], axis=0)
        kh = band_of(k_ref, kp_ref, b, g)
        return lax.dot_general(kh, qs, (((1,), (1,)), ((), ())), preferred_element_type=F32)

    def finish(b, g, st):
        rows = slice(b * WINDOW, (b + 1) * WINDOW)
        heads = range(g * Q_PER_KV, (g + 1) * Q_PER_KV)
        sink = jnp.zeros((1, stacked), F32)
        for n, h in enumerate(heads):
            sink = jnp.where(lane_head == n, sink_ref[h] * LOG2E, sink)
        st = st + bias_ref[0 if b == 0 else 1]
        m = jnp.maximum(jnp.max(st, axis=0, keepdims=True), sink)
        p = jnp.exp2(st - m)
        denom = jnp.sum(p, axis=0, keepdims=True) + jnp.exp2(sink - m)
        vh = band_of(v_ref, vp_ref, b, g)
        ot = lax.dot_general(vh, p.astype(BF16), (((0,), (0,)), ((), ())), preferred_element_type=F32)
        ot = ot * (1.0 / denom)
        for n, h in enumerate(heads):
            o_ref[h * HEAD_DIM:(h + 1) * HEAD_DIM, rows] = ot[:, n * WINDOW:(n + 1) * WINDOW].astype(o_ref.dtype)

    pairs = [(b, g) for b in range(nblk) for g in range(N_KV_HEADS)]
    ahead, slots = 2, 3
    for n in range(ahead):
        st_ref[n] = scores(*pairs[n])
    for n, (b, g) in enumerate(pairs):
        if n + ahead < len(pairs):
            st_ref[(n + ahead) % slots] = scores(*pairs[n + ahead])
        finish(b, g, st_ref[n % slots])


def _swa(proj3, kv3, sinks, *, tq):
    b, t, _ = proj3.shape
    nblk = tq // WINDOW
    qcol = COL_Q // ATTN_WIDTH
    prev = lambda bi, i: jnp.maximum(i * nblk - 1, 0)
    return pl.pallas_call(
        functools.partial(_swa_kernel, nblk=nblk),
        grid=(b, t // tq),
        in_specs=[
            pl.BlockSpec(memory_space=pltpu.SMEM),
            pl.BlockSpec((None, tq, ATTN_WIDTH), lambda bi, i: (bi, i, qcol)),
            pl.BlockSpec((None, tq, KV_WIDTH), lambda bi, i: (bi, i, 0)),
            pl.BlockSpec((None, tq, KV_WIDTH), lambda bi, i: (bi, i, 1)),
            pl.BlockSpec((None, WINDOW, KV_WIDTH), lambda bi, i: (bi, prev(bi, i), 0)),
            pl.BlockSpec((None, WINDOW, KV_WIDTH), lambda bi, i: (bi, prev(bi, i), 1)),
        ],
        out_specs=pl.BlockSpec((None, ATTN_WIDTH, tq), lambda bi, i: (bi, 0, i)),
        out_shape=jax.ShapeDtypeStruct((b, ATTN_WIDTH, t), BF16),
        scratch_shapes=[pltpu.VMEM((2, 2 * WINDOW, Q_PER_KV * WINDOW), F32),
                        pltpu.VMEM((3, 2 * WINDOW, Q_PER_KV * WINDOW), F32)],
        compiler_params=_params(("parallel", "arbitrary")),
        name="swa",
    )(sinks, proj3, kv3, kv3, kv3, kv3)


def _mix_kernel(attn_ref, z_ref, gb_ref, gc_ref, zp_ref, gcp_ref, ga_ref, gg_ref, x_ref, cw_ref,
                wap_ref, wcp_ref, wmo_ref, o_ref, ext_ref, conv_ref, ya_ref, merged_ref,
                *, tm, tiles_per_seq, chunk):
    i = pl.program_id(0)
    cz = z_ref[...].astype(F32) * gc_ref[...].astype(F32)
    czp = zp_ref[...].astype(F32) * gcp_ref[...].astype(F32)
    czp = jnp.where(i % tiles_per_seq == 0, 0.0, czp)
    ext_ref[0:HALO, :] = czp
    ext_ref[HALO:, :] = cz
    cw = cw_ref[...]
    y = (cw[0:1, :] * ext_ref[HALO - 2:HALO - 2 + tm, :]
         + cw[1:2, :] * ext_ref[HALO - 1:HALO - 1 + tm, :]
         + cw[2:3, :] * cz)
    conv_ref[...] = (gb_ref[...].astype(F32) * y).astype(BF16)
    for c in range(o_ref.shape[1] // chunk):
        cols = slice(c * chunk, (c + 1) * chunk)
        ya_ref[:, cols] = lax.dot_general(attn_ref[...], wap_ref[:, cols], (((0,), (0,)), ((), ())),
                                          preferred_element_type=F32)
        y_attn = ya_ref[:, cols]
        y_conv = jnp.dot(conv_ref[...], wcp_ref[:, cols], preferred_element_type=F32)
        t_attn = jnp.tanh(ga_ref[:, cols].astype(F32))
        t_conv = jnp.tanh(gg_ref[:, cols].astype(F32))
        merged_ref[:, cols] = ((y_attn + y_conv) + (t_attn * y_attn + t_conv * y_conv)).astype(BF16)
    o_ref[...] = x_ref[...] + jnp.dot(merged_ref[...], wmo_ref[...], preferred_element_type=F32)


def _mix(attn_t, proj, x2, conv_w, wap, wcp, wmo, seq, *, tm, chunk):
    n, d = x2.shape
    cwid = wcp.shape[0]
    tps = seq // tm
    zc, gbc, gcc = COL_Z // cwid, COL_GB // cwid, COL_GC // cwid
    gac, ggc = COL_GATE_A // d, COL_GATE_C // d
    rb = tm // HALO
    const = lambda i: (0, 0)
    return pl.pallas_call(
        functools.partial(_mix_kernel, tm=tm, tiles_per_seq=tps, chunk=chunk),
        grid=(n // tm,),
        in_specs=[
            pl.BlockSpec((None, ATTN_WIDTH, tm), lambda i: (i // tps, 0, i % tps)),
            pl.BlockSpec((tm, cwid), lambda i: (i, zc)),
            pl.BlockSpec((tm, cwid), lambda i: (i, gbc)),
            pl.BlockSpec((tm, cwid), lambda i: (i, gcc)),
            pl.BlockSpec((HALO, cwid), lambda i: (jnp.maximum(i * rb - 1, 0), zc)),
            pl.BlockSpec((HALO, cwid), lambda i: (jnp.maximum(i * rb - 1, 0), gcc)),
            pl.BlockSpec((tm, d), lambda i: (i, gac)),
            pl.BlockSpec((tm, d), lambda i: (i, ggc)),
            pl.BlockSpec((tm, d), lambda i: (i, 0)),
            pl.BlockSpec((CONV_K, cwid), const),
            pl.BlockSpec(wap.shape, const),
            pl.BlockSpec(wcp.shape, const),
            pl.BlockSpec(wmo.shape, const),
        ],
        out_specs=pl.BlockSpec((tm, d), lambda i: (i, 0)),
        out_shape=jax.ShapeDtypeStruct((n, d), F32),
        scratch_shapes=[pltpu.VMEM((tm + HALO, cwid), F32), pltpu.VMEM((tm, cwid), BF16),
                        pltpu.VMEM((tm, d), F32), pltpu.VMEM((tm, d), BF16)],
        compiler_params=_params(("parallel",)),
        name="mix",
    )(attn_t, proj, proj, proj, proj, proj, proj, proj, x2, conv_w, wap, wcp, wmo)


def _memkv_kernel(m_ref, g_ref, w_ref, o_ref):
    u = _rms(m_ref[...], g_ref[...]).astype(BF16)
    o_ref[...] = jnp.dot(u, w_ref[...], preferred_element_type=F32).astype(o_ref.dtype)


def _memkv(mem2, g, w, *, tm):
    n, d = mem2.shape
    return pl.pallas_call(
        _memkv_kernel,
        grid=(n // tm,),
        in_specs=[
            pl.BlockSpec((tm, d), lambda i: (i, 0)),
            pl.BlockSpec((1, d), lambda i: (0, 0)),
            pl.BlockSpec(w.shape, lambda i: (0, 0)),
        ],
        out_specs=pl.BlockSpec((tm, w.shape[1]), lambda i: (i, 0)),
        out_shape=jax.ShapeDtypeStruct((n, w.shape[1]), BF16),
        compiler_params=_params(("parallel",)),
        name="memkv",
    )(mem2, g, w)


def _xattn_kernel(h_ref, g_ref, wq_ref, kv_ref, wo_ref, o_ref):
    h = h_ref[...]
    u = _rms(h, g_ref[...]).astype(BF16)
    q = jnp.dot(u, wq_ref[...], preferred_element_type=F32) * (X_HEAD_DIM ** -0.5)
    q = q.astype(BF16)
    xw = X_HEADS * X_HEAD_DIM
    outs = []
    for hd in range(X_HEADS):
        cols = slice(hd * X_HEAD_DIM, (hd + 1) * X_HEAD_DIM)
        kh = kv_ref[:, cols]
        vh = kv_ref[:, xw + hd * X_HEAD_DIM:xw + (hd + 1) * X_HEAD_DIM]
        s = lax.dot_general(q[:, cols], kh, (((1,), (1,)), ((), ())), preferred_element_type=F32)
        m = jnp.max(s, axis=-1, keepdims=True)
        p = jnp.exp(s - m)
        denom = jnp.sum(p, axis=-1, keepdims=True)
        outs.append((jnp.dot(p.astype(BF16), vh, preferred_element_type=F32) / denom).astype(BF16))
    o = jnp.concatenate(outs, axis=1)
    o_ref[...] = h + jnp.dot(o, wo_ref[...], preferred_element_type=F32)


def _xattn(h3, g, wq, kv3, wo, *, tm):
    b, t, d = h3.shape
    m, kvw = kv3.shape[1], kv3.shape[2]
    return pl.pallas_call(
        _xattn_kernel,
        grid=(b, t // tm),
        in_specs=[
            pl.BlockSpec((None, tm, d), lambda bi, i: (bi, i, 0)),
            pl.BlockSpec((1, d), lambda bi, i: (0, 0)),
            pl.BlockSpec(wq.shape, lambda bi, i: (0, 0)),
            pl.BlockSpec((None, m, kvw), lambda bi, i: (bi, 0, 0)),
            pl.BlockSpec(wo.shape, lambda bi, i: (0, 0)),
        ],
        out_specs=pl.BlockSpec((None, tm, d), lambda bi, i: (bi, i, 0)),
        out_shape=jax.ShapeDtypeStruct((b, t, d), F32),
        compiler_params=_params(("parallel", "parallel")),
        name="xattn",
    )(h3, g, wq, kv3, wo)


def _ffn_kernel(h_ref, g_ref, wa_ref, wb_ref, wo_ref, gf_ref, o_ref, u_ref, act_ref, *, sub, chunk):
    k = pl.program_id(1)

    @pl.when(k == 0)
    def _():
        h = h_ref[...]
        u_ref[...] = _rms(h, g_ref[...]).astype(BF16)
        o_ref[...] = h

    th = act_ref.shape[1]
    for s in range(th // sub):
        hid = slice(s * sub, (s + 1) * sub)
        a = jnp.dot(u_ref[...], wa_ref[:, hid], preferred_element_type=F32)
        b = jnp.dot(u_ref[...], wb_ref[:, hid], preferred_element_type=F32)
        act_ref[:, hid] = (a * (1.0 + jnp.tanh(a)) * b).astype(BF16)
    for c in range(o_ref.shape[1] // chunk):
        cols = slice(c * chunk, (c + 1) * chunk)
        o_ref[:, cols] += jnp.dot(act_ref[...], wo_ref[:, cols], preferred_element_type=F32)

    @pl.when(k == pl.num_programs(1) - 1)
    def _():
        o_ref[...] = _rms(o_ref[...], gf_ref[...])


def _ffn(h2, g, w_in, w_out, g_final, *, tm, th, sub, chunk):
    n, d = h2.shape
    hidden = w_out.shape[0]
    nk = hidden // th
    return pl.pallas_call(
        functools.partial(_ffn_kernel, sub=sub, chunk=chunk),
        grid=(n // tm, nk),
        in_specs=[
            pl.BlockSpec((tm, d), lambda i, k: (i, 0)),
            pl.BlockSpec((1, d), lambda i, k: (0, 0)),
            pl.BlockSpec((d, th), lambda i, k: (0, k)),
            pl.BlockSpec((d, th), lambda i, k: (0, nk + k)),
            pl.BlockSpec((th, d), lambda i, k: (k, 0)),
            pl.BlockSpec((1, d), lambda i, k: (0, 0)),
        ],
        out_specs=pl.BlockSpec((tm, d), lambda i, k: (i, 0)),
        out_shape=jax.ShapeDtypeStruct((n, d), F32),
        scratch_shapes=[pltpu.VMEM((tm, d), BF16), pltpu.VMEM((tm, th), BF16)],
        compiler_params=_params(("parallel", "arbitrary")),
        name="ffn",
    )(h2, g, w_in, w_in, w_out, g_final)


def _rope_tables(seq):
    half = HEAD_DIM // 2
    inv_freq = ROPE_THETA ** (-jnp.arange(half, dtype=F32) / half)
    ang = jnp.arange(seq, dtype=jnp.int32).astype(F32)[:, None] * inv_freq[None, :]
    cos, sin = jnp.cos(ang), jnp.sin(ang)
    reps = LANES // HEAD_DIM
    cos_t = jnp.tile(jnp.concatenate([cos, cos], axis=1), (1, reps))
    sin_t = jnp.tile(jnp.concatenate([-sin, sin], axis=1), (1, reps))
    return cos_t, sin_t


def _cast_w_in(w):
    col_scale = jnp.where(jnp.arange(w.shape[1]) >= REF_GATE_A, 0.5, 1.0).astype(F32)
    return (w * col_scale).astype(BF16)


def _cast_w_ffn_in(w):
    two_hidden = w.shape[1]
    col_scale = jnp.where(jnp.arange(two_hidden) < two_hidden // 2, 0.5, 1.0).astype(F32)
    return (w * col_scale).astype(BF16)


def kernel(x, mem, g_mix, w_in, conv_w, attn_sinks, w_attn_proj, w_conv_proj, w_mix_out, g_xattn, g_mem,
           w_xq, w_xkv, w_xo, g_ffn, w_ffn_in, w_ffn_out, g_final):
    b, t, d = x.shape
    depth = w_in.shape[0]
    cos_t, sin_t = _rope_tables(t)
    h = x.reshape(b * t, d)
    mem2 = mem.reshape(b * mem.shape[1], d)
    for l in range(depth):
        w_in_bf = _cast_w_in(w_in[l])
        u, kv = _norm_kv(h, g_mix[l][None, :], w_in_bf, cos_t, sin_t, t, tm=512)
        q_scale = HEAD_DIM ** -0.5 * LOG2E
        proj = _inproj(u, w_in_bf, cos_t * q_scale, sin_t * q_scale, t, tm=1024, tn=2048, chunk=512)
        attn = _swa(proj.reshape(b, t, -1), kv.reshape(b, t, -1), attn_sinks[l], tq=512)
        h = _mix(attn, proj, h, conv_w[l], w_attn_proj[l].astype(BF16),
                 w_conv_proj[l].astype(BF16), (0.5 * w_mix_out[l]).astype(BF16), t, tm=256, chunk=1024)
        mkv = _memkv(mem2, g_mem[l][None, :], w_xkv[l].astype(BF16), tm=256)
        h = _xattn(h.reshape(b, t, d), g_xattn[l][None, :], w_xq[l].astype(BF16),
                   mkv.reshape(b, mem.shape[1], -1), w_xo[l].astype(BF16), tm=512).reshape(b * t, d)
        last = l == depth - 1
        assert last, "fused final norm expects a single layer"
        h = _ffn(h, g_ffn[l][None, :], _cast_w_ffn_in(w_ffn_in[l]), w_ffn_out[l].astype(BF16),
                 g_final[None, :], tm=1024, th=512, sub=256, chunk=512)
    return h.reshape(b, t, d)
```

```python
import functools

import jax
import jax.numpy as jnp
import numpy as np
from jax import lax
from jax.experimental import pallas as pl
from jax.experimental.pallas import tpu as pltpu

F32 = jnp.float32
BF16 = jnp.bfloat16

HEAD_DIM = 64
N_Q_HEADS = 16
N_KV_HEADS = 4
Q_PER_KV = N_Q_HEADS // N_KV_HEADS
ATTN_WIDTH = N_Q_HEADS * HEAD_DIM
KV_WIDTH = N_KV_HEADS * HEAD_DIM
WINDOW = 128
ROPE_THETA = 10000.0
CONV_K = 3
X_HEADS = 4
X_HEAD_DIM = 128
EPS = 1e-6
LOG2E = 1.4426950408889634

D_MODEL = 2048
CONV_WIDTH = 1024
REF_Q = 0
REF_K = REF_Q + ATTN_WIDTH
REF_V = REF_K + KV_WIDTH
REF_Z = REF_V + KV_WIDTH
REF_GB = REF_Z + CONV_WIDTH
REF_GC = REF_GB + CONV_WIDTH
REF_GATE_A = REF_GC + CONV_WIDTH
REF_GATE_C = REF_GATE_A + D_MODEL
COL_GATE_A = 0
COL_GATE_C = COL_GATE_A + D_MODEL
COL_Z = COL_GATE_C + D_MODEL
COL_GB = COL_Z + CONV_WIDTH
COL_GC = COL_GB + CONV_WIDTH
COL_Q = COL_GC + CONV_WIDTH
MAIN_WIDTH = COL_Q + ATTN_WIDTH
_MAIN_SRC = ((REF_GATE_A, D_MODEL), (REF_GATE_C, D_MODEL), (REF_Z, CONV_WIDTH), (REF_GB, CONV_WIDTH),
             (REF_GC, CONV_WIDTH), (REF_Q, ATTN_WIDTH))

LANES = 128
HALO = 16
VMEM_LIMIT = 56 * 1024 * 1024


def _rms(x, g):
    ms = jnp.mean(x * x, axis=-1, keepdims=True)
    return x * lax.rsqrt(ms + EPS) * g


def _params(sem):
    return pltpu.CompilerParams(dimension_semantics=sem, vmem_limit_bytes=VMEM_LIMIT)


def _rope(acc, cos, sin_signed):
    lane = lax.broadcasted_iota(jnp.int32, (1, LANES), 1)
    first_half = (lane % HEAD_DIM) < (HEAD_DIM // 2)
    outs = []
    for c in range(acc.shape[1] // LANES):
        xc = acc[:, c * LANES:(c + 1) * LANES]
        fwd = pltpu.roll(xc, LANES - HEAD_DIM // 2, axis=1)
        bwd = pltpu.roll(xc, HEAD_DIM // 2, axis=1)
        outs.append(xc * cos + jnp.where(first_half, fwd, bwd) * sin_signed)
    return jnp.concatenate(outs, axis=1)


def _norm_kv_kernel(x_ref, g_ref, w_ref, cos_ref, sin_ref, u_ref, kv_ref):
    u = _rms(x_ref[...], g_ref[...]).astype(BF16)
    u_ref[...] = u
    k = jnp.dot(u, w_ref[:, :KV_WIDTH], preferred_element_type=F32)
    kv_ref[:, :KV_WIDTH] = _rope(k, cos_ref[...], sin_ref[...]).astype(kv_ref.dtype)
    kv_ref[:, KV_WIDTH:] = jnp.dot(u, w_ref[:, KV_WIDTH:], preferred_element_type=F32).astype(kv_ref.dtype)


def _norm_kv(x2, g, w_in, cos_t, sin_t, seq, *, tm):
    n, d = x2.shape
    tpb = seq // tm
    assert REF_V == REF_K + KV_WIDTH and REF_K % (2 * KV_WIDTH) == 0
    return pl.pallas_call(
        _norm_kv_kernel,
        grid=(n // tm,),
        in_specs=[
            pl.BlockSpec((tm, d), lambda i: (i, 0)),
            pl.BlockSpec((1, d), lambda i: (0, 0)),
            pl.BlockSpec((d, 2 * KV_WIDTH), lambda i: (0, REF_K // (2 * KV_WIDTH))),
            pl.BlockSpec((tm, LANES), lambda i: (i % tpb, 0)),
            pl.BlockSpec((tm, LANES), lambda i: (i % tpb, 0)),
        ],
        out_specs=[
            pl.BlockSpec((tm, d), lambda i: (i, 0)),
            pl.BlockSpec((tm, 2 * KV_WIDTH), lambda i: (i, 0)),
        ],
        out_shape=[
            jax.ShapeDtypeStruct((n, d), BF16),
            jax.ShapeDtypeStruct((n, 2 * KV_WIDTH), BF16),
        ],
        compiler_params=_params(("parallel",)),
        name="norm_kv",
    )(x2, g, w_in, cos_t, sin_t)


def _inproj_kernel(u_ref, *refs, tn, chunk, side_scaled):
    n_chunks = tn // chunk
    w_refs, refs = refs[:n_chunks], refs[n_chunks:]
    (cos_ref, sin_ref), refs = refs[:2], refs[2:]
    n_side_in = len(side_scaled) + sum(side_scaled)
    side_in, (o_ref, *side_out) = list(refs[:n_side_in]), refs[n_side_in:]
    j = pl.program_id(0)
    for out_ref, scaled in zip(side_out, side_scaled):
        w = side_in.pop(0)[...]
        if scaled:
            w = w * side_in.pop(0)[...]
        out_ref[...] = w.astype(out_ref.dtype)
    n_tiles = MAIN_WIDTH // tn
    q_tile, q_lo = COL_Q // tn, COL_Q % tn
    assert q_lo % chunk == 0 and MAIN_WIDTH == (q_tile + 1) * tn

    def tile(rope_from):
        for c in range(n_chunks):
            acc = jnp.dot(u_ref[...], w_refs[c][...], preferred_element_type=F32)
            if c * chunk >= rope_from:
                acc = _rope(acc, cos_ref[...], sin_ref[...])
            o_ref[:, c * chunk:(c + 1) * chunk] = acc.astype(o_ref.dtype)

    pl.when(j < n_tiles - 1)(lambda: tile(tn))
    pl.when(j == n_tiles - 1)(lambda: tile(q_lo))


def _inproj(u, w_in, cos_t, sin_t, seq, side, *, tm, tn, chunk):
    n, d = u.shape
    tpb = seq // tm
    n_tiles, n_chunks = MAIN_WIDTH // tn, tn // chunk
    n_rows = n // tm

    side_in_specs, side_out_specs, side_out_shapes, side_args = [], [], [], []
    for w, scale, rows in side:
        n_blocks = w.shape[0] // rows
        assert w.shape[0] % rows == 0 and rows % HALO == 0 and n_blocks <= n_tiles * n_rows
        index = lambda j, i, n_blocks=n_blocks: (jnp.minimum(j * n_rows + i, n_blocks - 1), 0)
        side_in_specs.append(pl.BlockSpec((rows, w.shape[1]), index))
        side_args.append(w)
        if scale is not None:
            side_in_specs.append(pl.BlockSpec((1, w.shape[1]), lambda j, i: (0, 0)))
            side_args.append(scale)
        side_out_specs.append(pl.BlockSpec((rows, w.shape[1]), index))
        side_out_shapes.append(jax.ShapeDtypeStruct(w.shape, BF16))
    src_blocks = []
    for ref_off, width in _MAIN_SRC:
        assert ref_off % chunk == 0 and width % chunk == 0
        src_blocks += [ref_off // chunk + b for b in range(width // chunk)]
    assert len(src_blocks) == n_tiles * n_chunks

    def w_spec(c):
        def index(j, i):
            blk = 0
            for jj in range(n_tiles):
                blk = blk + jnp.where(j == jj, src_blocks[jj * n_chunks + c], 0)
            return (0, blk)
        return pl.BlockSpec((d, chunk), index)

    proj, *side_bf = pl.pallas_call(
        functools.partial(_inproj_kernel, tn=tn, chunk=chunk,
                          side_scaled=tuple(scale is not None for _, scale, _ in side)),
        grid=(n_tiles, n_rows),
        in_specs=[pl.BlockSpec((tm, d), lambda j, i: (i, 0))]
        + [w_spec(c) for c in range(n_chunks)]
        + [pl.BlockSpec((tm, LANES), lambda j, i: (i % tpb, 0)),
           pl.BlockSpec((tm, LANES), lambda j, i: (i % tpb, 0))]
        + side_in_specs,
        out_specs=[pl.BlockSpec((tm, tn), lambda j, i: (i, j))] + side_out_specs,
        out_shape=[jax.ShapeDtypeStruct((n, MAIN_WIDTH), BF16)] + side_out_shapes,
        compiler_params=_params(("arbitrary", "arbitrary")),
        name="inproj",
    )(u, *([w_in] * n_chunks), cos_t, sin_t, *side_args)
    return proj, side_bf


def _swa_kernel(sink_ref, q_ref, k_ref, v_ref, kp_ref, vp_ref, o_ref, bias_ref, st_ref, *, nblk):
    i = pl.program_id(1)
    stacked = Q_PER_KV * WINDOW
    key = lax.broadcasted_iota(jnp.int32, (2 * WINDOW, stacked), 0)
    qry = lax.broadcasted_iota(jnp.int32, (2 * WINDOW, stacked), 1) % WINDOW
    band = (key > qry) & (key <= qry + WINDOW)
    bias_ref[0] = jnp.where(band & ((key >= WINDOW) | (i > 0)), 0.0, -jnp.inf)
    bias_ref[1] = jnp.where(band, 0.0, -jnp.inf)
    lane_head = lax.broadcasted_iota(jnp.int32, (1, stacked), 1) // WINDOW

    def band_of(ref, prev_ref, b, g):
        cols = slice(g * HEAD_DIM, (g + 1) * HEAD_DIM)
        if b == 0:
            return jnp.concatenate([prev_ref[:, cols], ref[0:WINDOW, cols]], axis=0)
        return ref[(b - 1) * WINDOW:(b + 1) * WINDOW, cols]

    def scores(b, g):
        rows = slice(b * WINDOW, (b + 1) * WINDOW)
        heads = range(g * Q_PER_KV, (g + 1) * Q_PER_KV)
        qs = jnp.concatenate([q_ref[rows, h * HEAD_DIM:(h + 1) * HEAD_DIM] for h in heads], axis=0)
        kh = band_of(k_ref, kp_ref, b, g)
        return lax.dot_general(kh, qs, (((1,), (1,)), ((), ())), preferred_element_type=F32)

    def finish(b, g, st):
        rows = slice(b * WINDOW, (b + 1) * WINDOW)
        heads = range(g * Q_PER_KV, (g + 1) * Q_PER_KV)
        sink = jnp.zeros((1, stacked), F32)
        for n, h in enumerate(heads):
            sink = jnp.where(lane_head == n, sink_ref[h] * LOG2E, sink)
        st = st + bias_ref[0 if b == 0 else 1]
        m = jnp.maximum(jnp.max(st, axis=0, keepdims=True), sink)
        p = jnp.exp2(st - m)
        denom = jnp.sum(p, axis=0, keepdims=True) + jnp.exp2(sink - m)
        vh = band_of(v_ref, vp_ref, b, g)
        ot = lax.dot_general(vh, p.astype(BF16), (((0,), (0,)), ((), ())), preferred_element_type=F32)
        ot = ot * (1.0 / denom)
        for n, h in enumerate(heads):
            o_ref[h * HEAD_DIM:(h + 1) * HEAD_DIM, rows] = ot[:, n * WINDOW:(n + 1) * WINDOW].astype(o_ref.dtype)

    pairs = [(b, g) for b in range(nblk) for g in range(N_KV_HEADS)]
    ahead, slots = 2, 3
    for n in range(ahead):
        st_ref[n] = scores(*pairs[n])
    for n, (b, g) in enumerate(pairs):
        if n + ahead < len(pairs):
            st_ref[(n + ahead) % slots] = scores(*pairs[n + ahead])
        finish(b, g, st_ref[n % slots])


def _swa(proj3, kv3, sinks, *, tq):
    b, t, _ = proj3.shape
    nblk = tq // WINDOW
    qcol = COL_Q // ATTN_WIDTH
    prev = lambda bi, i: jnp.maximum(i * nblk - 1, 0)
    return pl.pallas_call(
        functools.partial(_swa_kernel, nblk=nblk),
        grid=(b, t // tq),
        in_specs=[
            pl.BlockSpec(memory_space=pltpu.SMEM),
            pl.BlockSpec((None, tq, ATTN_WIDTH), lambda bi, i: (bi, i, qcol)),
            pl.BlockSpec((None, tq, KV_WIDTH), lambda bi, i: (bi, i, 0)),
            pl.BlockSpec((None, tq, KV_WIDTH), lambda bi, i: (bi, i, 1)),
            pl.BlockSpec((None, WINDOW, KV_WIDTH), lambda bi, i: (bi, prev(bi, i), 0)),
            pl.BlockSpec((None, WINDOW, KV_WIDTH), lambda bi, i: (bi, prev(bi, i), 1)),
        ],
        out_specs=pl.BlockSpec((None, ATTN_WIDTH, tq), lambda bi, i: (bi, 0, i)),
        out_shape=jax.ShapeDtypeStruct((b, ATTN_WIDTH, t), BF16),
        scratch_shapes=[pltpu.VMEM((2, 2 * WINDOW, Q_PER_KV * WINDOW), F32),
                        pltpu.VMEM((3, 2 * WINDOW, Q_PER_KV * WINDOW), F32)],
        compiler_params=_params(("parallel", "arbitrary")),
        name="swa",
    )(sinks, proj3, kv3, kv3, kv3, kv3)


def _mix_kernel(attn_ref, z_ref, gb_ref, gc_ref, zp_ref, gcp_ref, ga_ref, gg_ref, x_ref, cw_ref,
                wap_ref, wcp_ref, wmo_ref, o_ref, ext_ref, conv_ref, ya_ref, merged_ref,
                *, tm, tiles_per_seq, chunk):
    i = pl.program_id(0)
    cz = z_ref[...].astype(F32) * gc_ref[...].astype(F32)
    czp = zp_ref[...].astype(F32) * gcp_ref[...].astype(F32)
    czp = jnp.where(i % tiles_per_seq == 0, 0.0, czp)
    ext_ref[0:HALO, :] = czp
    ext_ref[HALO:, :] = cz
    cw = cw_ref[...]
    y = (cw[0:1, :] * ext_ref[HALO - 2:HALO - 2 + tm, :]
         + cw[1:2, :] * ext_ref[HALO - 1:HALO - 1 + tm, :]
         + cw[2:3, :] * cz)
    conv_ref[...] = (gb_ref[...].astype(F32) * y).astype(BF16)
    for c in range(o_ref.shape[1] // chunk):
        cols = slice(c * chunk, (c + 1) * chunk)
        ya_ref[:, cols] = lax.dot_general(attn_ref[...], wap_ref[:, cols], (((0,), (0,)), ((), ())),
                                          preferred_element_type=F32)
        y_attn = ya_ref[:, cols]
        y_conv = jnp.dot(conv_ref[...], wcp_ref[:, cols], preferred_element_type=F32)
        t_attn = jnp.tanh(ga_ref[:, cols].astype(F32))
        t_conv = jnp.tanh(gg_ref[:, cols].astype(F32))
        merged_ref[:, cols] = ((y_attn + y_conv) + (t_attn * y_attn + t_conv * y_conv)).astype(BF16)
    o_ref[...] = x_ref[...] + jnp.dot(merged_ref[...], wmo_ref[...], preferred_element_type=F32)


def _mix(attn_t, proj, x2, conv_w, wap, wcp, wmo, seq, *, tm, chunk):
    n, d = x2.shape
    cwid = wcp.shape[0]
    tps = seq // tm
    zc, gbc, gcc = COL_Z // cwid, COL_GB // cwid, COL_GC // cwid
    gac, ggc = COL_GATE_A // d, COL_GATE_C // d
    rb = tm // HALO
    const = lambda i: (0, 0)
    return pl.pallas_call(
        functools.partial(_mix_kernel, tm=tm, tiles_per_seq=tps, chunk=chunk),
        grid=(n // tm,),
        in_specs=[
            pl.BlockSpec((None, ATTN_WIDTH, tm), lambda i: (i // tps, 0, i % tps)),
            pl.BlockSpec((tm, cwid), lambda i: (i, zc)),
            pl.BlockSpec((tm, cwid), lambda i: (i, gbc)),
            pl.BlockSpec((tm, cwid), lambda i: (i, gcc)),
            pl.BlockSpec((HALO, cwid), lambda i: (jnp.maximum(i * rb - 1, 0), zc)),
            pl.BlockSpec((HALO, cwid), lambda i: (jnp.maximum(i * rb - 1, 0), gcc)),
            pl.BlockSpec((tm, d), lambda i: (i, gac)),
            pl.BlockSpec((tm, d), lambda i: (i, ggc)),
            pl.BlockSpec((tm, d), lambda i: (i, 0)),
            pl.BlockSpec((CONV_K, cwid), const),
            pl.BlockSpec(wap.shape, const),
            pl.BlockSpec(wcp.shape, const),
            pl.BlockSpec(wmo.shape, const),
        ],
        out_specs=pl.BlockSpec((tm, d), lambda i: (i, 0)),
        out_shape=jax.ShapeDtypeStruct((n, d), F32),
        scratch_shapes=[pltpu.VMEM((tm + HALO, cwid), F32), pltpu.VMEM((tm, cwid), BF16),
                        pltpu.VMEM((tm, d), F32), pltpu.VMEM((tm, d), BF16)],
        compiler_params=_params(("parallel",)),
        name="mix",
    )(attn_t, proj, proj, proj, proj, proj, proj, proj, x2, conv_w, wap, wcp, wmo)


def _memkv_kernel(m_ref, g_ref, w_ref, o_ref):
    u = _rms(m_ref[...], g_ref[...]).astype(BF16)
    o_ref[...] = jnp.dot(u, w_ref[...], preferred_element_type=F32).astype(o_ref.dtype)


def _memkv(mem2, g, w, *, tm):
    n, d = mem2.shape
    return pl.pallas_call(
        _memkv_kernel,
        grid=(n // tm,),
        in_specs=[
            pl.BlockSpec((tm, d), lambda i: (i, 0)),
            pl.BlockSpec((1, d), lambda i: (0, 0)),
            pl.BlockSpec(w.shape, lambda i: (0, 0)),
        ],
        out_specs=pl.BlockSpec((tm, w.shape[1]), lambda i: (i, 0)),
        out_shape=jax.ShapeDtypeStruct((n, w.shape[1]), BF16),
        compiler_params=_params(("parallel",)),
        name="memkv",
    )(mem2, g, w)


def _xattn_kernel(h_ref, g_ref, wq_ref, kv_ref, wo_ref, o_ref):
    h = h_ref[...]
    u = _rms(h, g_ref[...]).astype(BF16)
    q = jnp.dot(u, wq_ref[...], preferred_element_type=F32) * (X_HEAD_DIM ** -0.5)
    q = q.astype(BF16)
    xw = X_HEADS * X_HEAD_DIM
    outs = []
    for hd in range(X_HEADS):
        cols = slice(hd * X_HEAD_DIM, (hd + 1) * X_HEAD_DIM)
        kh = kv_ref[:, cols]
        vh = kv_ref[:, xw + hd * X_HEAD_DIM:xw + (hd + 1) * X_HEAD_DIM]
        s = lax.dot_general(q[:, cols], kh, (((1,), (1,)), ((), ())), preferred_element_type=F32)
        m = jnp.max(s, axis=-1, keepdims=True)
        p = jnp.exp(s - m)
        denom = jnp.sum(p, axis=-1, keepdims=True)
        outs.append((jnp.dot(p.astype(BF16), vh, preferred_element_type=F32) / denom).astype(BF16))
    o = jnp.concatenate(outs, axis=1)
    o_ref[...] = h + jnp.dot(o, wo_ref[...], preferred_element_type=F32)


def _xattn(h3, g, wq, kv3, wo, *, tm):
    b, t, d = h3.shape
    m, kvw = kv3.shape[1], kv3.shape[2]
    return pl.pallas_call(
        _xattn_kernel,
        grid=(b, t // tm),
        in_specs=[
            pl.BlockSpec((None, tm, d), lambda bi, i: (bi, i, 0)),
            pl.BlockSpec((1, d), lambda bi, i: (0, 0)),
            pl.BlockSpec(wq.shape, lambda bi, i: (0, 0)),
            pl.BlockSpec((None, m, kvw), lambda bi, i: (bi, 0, 0)),
            pl.BlockSpec(wo.shape, lambda bi, i: (0, 0)),
        ],
        out_specs=pl.BlockSpec((None, tm, d), lambda bi, i: (bi, i, 0)),
        out_shape=jax.ShapeDtypeStruct((b, t, d), F32),
        compiler_params=_params(("parallel", "parallel")),
        name="xattn",
    )(h3, g, wq, kv3, wo)


def _ffn_kernel(h_ref, g_ref, wa_ref, wb_ref, wo_ref, gf_ref, o_ref, u_ref, act_ref, *, sub, chunk):
    k = pl.program_id(1)

    @pl.when(k == 0)
    def _():
        h = h_ref[...]
        u_ref[...] = _rms(h, g_ref[...]).astype(BF16)
        o_ref[...] = h

    th = act_ref.shape[1]
    for s in range(th // sub):
        hid = slice(s * sub, (s + 1) * sub)
        a = jnp.dot(u_ref[...], wa_ref[:, hid], preferred_element_type=F32)
        b = jnp.dot(u_ref[...], wb_ref[:, hid], preferred_element_type=F32)
        act_ref[:, hid] = (a * (1.0 + jnp.tanh(a)) * b).astype(BF16)
    for c in range(o_ref.shape[1] // chunk):
        cols = slice(c * chunk, (c + 1) * chunk)
        o_ref[:, cols] += jnp.dot(act_ref[...], wo_ref[:, cols], preferred_element_type=F32)

    @pl.when(k == pl.num_programs(1) - 1)
    def _():
        o_ref[...] = _rms(o_ref[...], gf_ref[...])


def _ffn(h2, g, w_in, w_out, g_final, *, tm, th, sub, chunk):
    n, d = h2.shape
    hidden = w_out.shape[0]
    nk = hidden // th
    return pl.pallas_call(
        functools.partial(_ffn_kernel, sub=sub, chunk=chunk),
        grid=(n // tm, nk),
        in_specs=[
            pl.BlockSpec((tm, d), lambda i, k: (i, 0)),
            pl.BlockSpec((1, d), lambda i, k: (0, 0)),
            pl.BlockSpec((d, th), lambda i, k: (0, k)),
            pl.BlockSpec((d, th), lambda i, k: (0, nk + k)),
            pl.BlockSpec((th, d), lambda i, k: (k, 0)),
            pl.BlockSpec((1, d), lambda i, k: (0, 0)),
        ],
        out_specs=pl.BlockSpec((tm, d), lambda i, k: (i, 0)),
        out_shape=jax.ShapeDtypeStruct((n, d), F32),
        scratch_shapes=[pltpu.VMEM((tm, d), BF16), pltpu.VMEM((tm, th), BF16)],
        compiler_params=_params(("parallel", "arbitrary")),
        name="ffn",
    )(h2, g, w_in, w_in, w_out, g_final)


def _rope_tables(seq):
    half = HEAD_DIM // 2
    inv_freq = ROPE_THETA ** (-jnp.arange(half, dtype=F32) / half)
    ang = jnp.arange(seq, dtype=jnp.int32).astype(F32)[:, None] * inv_freq[None, :]
    cos, sin = jnp.cos(ang), jnp.sin(ang)
    reps = LANES // HEAD_DIM
    cos_t = jnp.tile(jnp.concatenate([cos, cos], axis=1), (1, reps))
    sin_t = jnp.tile(jnp.concatenate([-sin, sin], axis=1), (1, reps))
    return cos_t, sin_t


def _cast_w_in(w):
    col_scale = jnp.where(jnp.arange(w.shape[1]) >= REF_GATE_A, 0.5, 1.0).astype(F32)
    return (w * col_scale).astype(BF16)


def _col_scale(cols, value, lo=0, hi=None):
    idx = jnp.arange(cols)
    hi = cols if hi is None else hi
    return jnp.where((idx >= lo) & (idx < hi), value, 1.0).astype(F32)[None, :]


def kernel(x, mem, g_mix, w_in, conv_w, attn_sinks, w_attn_proj, w_conv_proj, w_mix_out, g_xattn, g_mem,
           w_xq, w_xkv, w_xo, g_ffn, w_ffn_in, w_ffn_out, g_final):
    b, t, d = x.shape
    depth = w_in.shape[0]
    cos_t, sin_t = _rope_tables(t)
    h = x.reshape(b * t, d)
    mem2 = mem.reshape(b * mem.shape[1], d)
    for l in range(depth):
        w_in_bf = _cast_w_in(w_in[l])
        u, kv = _norm_kv(h, g_mix[l][None, :], w_in_bf, cos_t, sin_t, t, tm=512)
        q_scale = HEAD_DIM ** -0.5 * LOG2E
        hidden = w_ffn_out.shape[1]
        side = [
            (w_ffn_in[l], _col_scale(2 * hidden, 0.5, 0, hidden), 32),
            (w_ffn_out[l], None, 128),
            (w_mix_out[l], _col_scale(d, 0.5), 32),
            (w_attn_proj[l], None, 16),
            (w_conv_proj[l], None, 16),
            (w_xq[l], None, 32),
            (w_xkv[l], None, 32),
            (w_xo[l], None, 16),
        ]
        proj, side_bf = _inproj(u, w_in_bf, cos_t * q_scale, sin_t * q_scale, t, side,
                                tm=1024, tn=2048, chunk=512)
        (w_ffn_in_bf, w_ffn_out_bf, w_mix_out_bf, w_attn_proj_bf, w_conv_proj_bf,
         w_xq_bf, w_xkv_bf, w_xo_bf) = side_bf
        attn = _swa(proj.reshape(b, t, -1), kv.reshape(b, t, -1), attn_sinks[l], tq=512)
        h = _mix(attn, proj, h, conv_w[l], w_attn_proj_bf, w_conv_proj_bf, w_mix_out_bf, t, tm=256, chunk=1024)
        mkv = _memkv(mem2, g_mem[l][None, :], w_xkv_bf, tm=256)
        h = _xattn(h.reshape(b, t, d), g_xattn[l][None, :], w_xq_bf,
                   mkv.reshape(b, mem.shape[1], -1), w_xo_bf, tm=512).reshape(b * t, d)
        last = l == depth - 1
        assert last, "fused final norm expects a single layer"
        h = _ffn(h, g_ffn[l][None, :], w_ffn_in_bf, w_ffn_out_bf,
                 g_final[None, :], tm=1024, th=512, sub=256, chunk=512)
    return h.reshape(b, t, d)
```

```python
import functools

import jax
import jax.numpy as jnp
import numpy as np
from jax import lax
from jax.experimental import pallas as pl
from jax.experimental.pallas import tpu as pltpu

F32 = jnp.float32
BF16 = jnp.bfloat16

HEAD_DIM = 64
N_Q_HEADS = 16
N_KV_HEADS = 4
Q_PER_KV = N_Q_HEADS // N_KV_HEADS
ATTN_WIDTH = N_Q_HEADS * HEAD_DIM
KV_WIDTH = N_KV_HEADS * HEAD_DIM
WINDOW = 128
ROPE_THETA = 10000.0
CONV_K = 3
X_HEADS = 4
X_HEAD_DIM = 128
EPS = 1e-6
LOG2E = 1.4426950408889634

D_MODEL = 2048
CONV_WIDTH = 1024
REF_Q = 0
REF_K = REF_Q + ATTN_WIDTH
REF_V = REF_K + KV_WIDTH
REF_Z = REF_V + KV_WIDTH
REF_GB = REF_Z + CONV_WIDTH
REF_GC = REF_GB + CONV_WIDTH
REF_GATE_A = REF_GC + CONV_WIDTH
REF_GATE_C = REF_GATE_A + D_MODEL
COL_GATE_A = 0
COL_GATE_C = COL_GATE_A + D_MODEL
COL_Z = COL_GATE_C + D_MODEL
COL_GB = COL_Z + CONV_WIDTH
COL_GC = COL_GB + CONV_WIDTH
COL_Q = COL_GC + CONV_WIDTH
MAIN_WIDTH = COL_Q + ATTN_WIDTH
_MAIN_SRC = ((REF_GATE_A, D_MODEL), (REF_GATE_C, D_MODEL), (REF_Z, CONV_WIDTH), (REF_GB, CONV_WIDTH),
             (REF_GC, CONV_WIDTH), (REF_Q, ATTN_WIDTH))

LANES = 128
HALO = 16
VMEM_LIMIT = 56 * 1024 * 1024


def _rms(x, g):
    ms = jnp.mean(x * x, axis=-1, keepdims=True)
    return x * lax.rsqrt(ms + EPS) * g


def _params(sem):
    return pltpu.CompilerParams(dimension_semantics=sem, vmem_limit_bytes=VMEM_LIMIT)


def _rope(acc, cos, sin_signed):
    lane = lax.broadcasted_iota(jnp.int32, (1, LANES), 1)
    first_half = (lane % HEAD_DIM) < (HEAD_DIM // 2)
    outs = []
    for c in range(acc.shape[1] // LANES):
        xc = acc[:, c * LANES:(c + 1) * LANES]
        fwd = pltpu.roll(xc, LANES - HEAD_DIM // 2, axis=1)
        bwd = pltpu.roll(xc, HEAD_DIM // 2, axis=1)
        outs.append(xc * cos + jnp.where(first_half, fwd, bwd) * sin_signed)
    return jnp.concatenate(outs, axis=1)


def _norm_kv_kernel(x_ref, g_ref, w_ref, cos_ref, sin_ref, u_ref, kv_ref):
    u = _rms(x_ref[...], g_ref[...]).astype(BF16)
    u_ref[...] = u
    k = jnp.dot(u, w_ref[:, :KV_WIDTH], preferred_element_type=F32)
    kv_ref[:, :KV_WIDTH] = _rope(k, cos_ref[...], sin_ref[...]).astype(kv_ref.dtype)
    kv_ref[:, KV_WIDTH:] = jnp.dot(u, w_ref[:, KV_WIDTH:], preferred_element_type=F32).astype(kv_ref.dtype)


def _norm_kv(x2, g, w_in, cos_t, sin_t, seq, *, tm):
    n, d = x2.shape
    tpb = seq // tm
    assert REF_V == REF_K + KV_WIDTH and REF_K % (2 * KV_WIDTH) == 0
    return pl.pallas_call(
        _norm_kv_kernel,
        grid=(n // tm,),
        in_specs=[
            pl.BlockSpec((tm, d), lambda i: (i, 0)),
            pl.BlockSpec((1, d), lambda i: (0, 0)),
            pl.BlockSpec((d, 2 * KV_WIDTH), lambda i: (0, REF_K // (2 * KV_WIDTH))),
            pl.BlockSpec((tm, LANES), lambda i: (i % tpb, 0)),
            pl.BlockSpec((tm, LANES), lambda i: (i % tpb, 0)),
        ],
        out_specs=[
            pl.BlockSpec((tm, d), lambda i: (i, 0)),
            pl.BlockSpec((tm, 2 * KV_WIDTH), lambda i: (i, 0)),
        ],
        out_shape=[
            jax.ShapeDtypeStruct((n, d), BF16),
            jax.ShapeDtypeStruct((n, 2 * KV_WIDTH), BF16),
        ],
        compiler_params=_params(("parallel",)),
        name="norm_kv",
    )(x2, g, w_in, cos_t, sin_t)


def _inproj_kernel(u_ref, *refs, tn, chunk, side_scaled):
    n_chunks = tn // chunk
    w_refs, refs = refs[:n_chunks], refs[n_chunks:]
    (cos_ref, sin_ref), refs = refs[:2], refs[2:]
    n_side_in = len(side_scaled) + sum(side_scaled)
    side_in, (o_ref, *side_out) = list(refs[:n_side_in]), refs[n_side_in:]
    j = pl.program_id(0)
    for out_ref, scaled in zip(side_out, side_scaled):
        w = side_in.pop(0)[...]
        if scaled:
            w = w * side_in.pop(0)[...]
        out_ref[...] = w.astype(out_ref.dtype)
    n_tiles = MAIN_WIDTH // tn
    q_tile, q_lo = COL_Q // tn, COL_Q % tn
    assert q_lo % chunk == 0 and MAIN_WIDTH == (q_tile + 1) * tn

    def tile(rope_from):
        for c in range(n_chunks):
            acc = jnp.dot(u_ref[...], w_refs[c][...], preferred_element_type=F32)
            if c * chunk >= rope_from:
                acc = _rope(acc, cos_ref[...], sin_ref[...])
            o_ref[:, c * chunk:(c + 1) * chunk] = acc.astype(o_ref.dtype)

    pl.when(j < n_tiles - 1)(lambda: tile(tn))
    pl.when(j == n_tiles - 1)(lambda: tile(q_lo))


def _inproj(u, w_in, cos_t, sin_t, seq, side, *, tm, tn, chunk):
    n, d = u.shape
    tpb = seq // tm
    n_tiles, n_chunks = MAIN_WIDTH // tn, tn // chunk
    n_rows = n // tm

    side_in_specs, side_out_specs, side_out_shapes, side_args = [], [], [], []
    for w, scale, rows in side:
        n_blocks = w.shape[0] // rows
        assert w.shape[0] % rows == 0 and rows % HALO == 0 and n_blocks <= n_tiles * n_rows
        index = lambda j, i, n_blocks=n_blocks: (jnp.minimum(j * n_rows + i, n_blocks - 1), 0)
        side_in_specs.append(pl.BlockSpec((rows, w.shape[1]), index))
        side_args.append(w)
        if scale is not None:
            side_in_specs.append(pl.BlockSpec((1, w.shape[1]), lambda j, i: (0, 0)))
            side_args.append(scale)
        side_out_specs.append(pl.BlockSpec((rows, w.shape[1]), index))
        side_out_shapes.append(jax.ShapeDtypeStruct(w.shape, BF16))
    src_blocks = []
    for ref_off, width in _MAIN_SRC:
        assert ref_off % chunk == 0 and width % chunk == 0
        src_blocks += [ref_off // chunk + b for b in range(width // chunk)]
    assert len(src_blocks) == n_tiles * n_chunks

    def w_spec(c):
        def index(j, i):
            blk = 0
            for jj in range(n_tiles):
                blk = blk + jnp.where(j == jj, src_blocks[jj * n_chunks + c], 0)
            return (0, blk)
        return pl.BlockSpec((d, chunk), index)

    proj, *side_bf = pl.pallas_call(
        functools.partial(_inproj_kernel, tn=tn, chunk=chunk,
                          side_scaled=tuple(scale is not None for _, scale, _ in side)),
        grid=(n_tiles, n_rows),
        in_specs=[pl.BlockSpec((tm, d), lambda j, i: (i, 0))]
        + [w_spec(c) for c in range(n_chunks)]
        + [pl.BlockSpec((tm, LANES), lambda j, i: (i % tpb, 0)),
           pl.BlockSpec((tm, LANES), lambda j, i: (i % tpb, 0))]
        + side_in_specs,
        out_specs=[pl.BlockSpec((tm, tn), lambda j, i: (i, j))] + side_out_specs,
        out_shape=[jax.ShapeDtypeStruct((n, MAIN_WIDTH), BF16)] + side_out_shapes,
        compiler_params=_params(("arbitrary", "arbitrary")),
        name="inproj",
    )(u, *([w_in] * n_chunks), cos_t, sin_t, *side_args)
    return proj, side_bf


def _swa_kernel(sink_ref, q_ref, k_ref, v_ref, kp_ref, vp_ref, o_ref, bias_ref, st_ref, *, nblk):
    i = pl.program_id(1)
    stacked = Q_PER_KV * WINDOW
    key = lax.broadcasted_iota(jnp.int32, (2 * WINDOW, stacked), 0)
    qry = lax.broadcasted_iota(jnp.int32, (2 * WINDOW, stacked), 1) % WINDOW
    band = (key > qry) & (key <= qry + WINDOW)
    bias_ref[0] = jnp.where(band & ((key >= WINDOW) | (i > 0)), 0.0, -jnp.inf)
    bias_ref[1] = jnp.where(band, 0.0, -jnp.inf)
    lane_head = lax.broadcasted_iota(jnp.int32, (1, stacked), 1) // WINDOW

    def band_of(ref, prev_ref, b, g):
        cols = slice(g * HEAD_DIM, (g + 1) * HEAD_DIM)
        if b == 0:
            return jnp.concatenate([prev_ref[:, cols], ref[0:WINDOW, cols]], axis=0)
        return ref[(b - 1) * WINDOW:(b + 1) * WINDOW, cols]

    def scores(b, g):
        rows = slice(b * WINDOW, (b + 1) * WINDOW)
        heads = range(g * Q_PER_KV, (g + 1) * Q_PER_KV)
        qs = jnp.concatenate([q_ref[rows, h * HEAD_DIM:(h + 1) * HEAD_DIM] for h in heads], axis=0)
        kh = band_of(k_ref, kp_ref, b, g)
        return lax.dot_general(kh, qs, (((1,), (1,)), ((), ())), preferred_element_type=F32)

    def finish(b, g, st):
        rows = slice(b * WINDOW, (b + 1) * WINDOW)
        heads = range(g * Q_PER_KV, (g + 1) * Q_PER_KV)
        sink = jnp.zeros((1, stacked), F32)
        for n, h in enumerate(heads):
            sink = jnp.where(lane_head == n, sink_ref[h] * LOG2E, sink)
        st = st + bias_ref[0 if b == 0 else 1]
        m = jnp.maximum(jnp.max(st, axis=0, keepdims=True), sink)
        p = jnp.exp2(st - m)
        denom = jnp.sum(p, axis=0, keepdims=True) + jnp.exp2(sink - m)
        vh = band_of(v_ref, vp_ref, b, g)
        ot = lax.dot_general(vh, p.astype(BF16), (((0,), (0,)), ((), ())), preferred_element_type=F32)
        ot = ot * (1.0 / denom)
        for n, h in enumerate(heads):
            o_ref[h * HEAD_DIM:(h + 1) * HEAD_DIM, rows] = ot[:, n * WINDOW:(n + 1) * WINDOW].astype(o_ref.dtype)

    pairs = [(b, g) for b in range(nblk) for g in range(N_KV_HEADS)]
    ahead, slots = 2, 3
    for n in range(ahead):
        st_ref[n] = scores(*pairs[n])
    for n, (b, g) in enumerate(pairs):
        if n + ahead < len(pairs):
            st_ref[(n + ahead) % slots] = scores(*pairs[n + ahead])
        finish(b, g, st_ref[n % slots])


def _swa(proj3, kv3, sinks, *, tq):
    b, t, _ = proj3.shape
    nblk = tq // WINDOW
    qcol = COL_Q // ATTN_WIDTH
    prev = lambda bi, i: jnp.maximum(i * nblk - 1, 0)
    return pl.pallas_call(
        functools.partial(_swa_kernel, nblk=nblk),
        grid=(b, t // tq),
        in_specs=[
            pl.BlockSpec(memory_space=pltpu.SMEM),
            pl.BlockSpec((None, tq, ATTN_WIDTH), lambda bi, i: (bi, i, qcol)),
            pl.BlockSpec((None, tq, KV_WIDTH), lambda bi, i: (bi, i, 0)),
            pl.BlockSpec((None, tq, KV_WIDTH), lambda bi, i: (bi, i, 1)),
            pl.BlockSpec((None, WINDOW, KV_WIDTH), lambda bi, i: (bi, prev(bi, i), 0)),
            pl.BlockSpec((None, WINDOW, KV_WIDTH), lambda bi, i: (bi, prev(bi, i), 1)),
        ],
        out_specs=pl.BlockSpec((None, ATTN_WIDTH, tq), lambda bi, i: (bi, 0, i)),
        out_shape=jax.ShapeDtypeStruct((b, ATTN_WIDTH, t), BF16),
        scratch_shapes=[pltpu.VMEM((2, 2 * WINDOW, Q_PER_KV * WINDOW), F32),
                        pltpu.VMEM((3, 2 * WINDOW, Q_PER_KV * WINDOW), F32)],
        compiler_params=_params(("parallel", "arbitrary")),
        name="swa",
    )(sinks, proj3, kv3, kv3, kv3, kv3)


def _mix_kernel(attn_ref, z_ref, gb_ref, gc_ref, zp_ref, gcp_ref, ga_ref, gg_ref, x_ref, cw_ref,
                wap_ref, wcp_ref, wmo_ref, o_ref, ext_ref, conv_ref, ya_ref, merged_ref,
                *, tm, tiles_per_seq, chunk):
    i = pl.program_id(0)
    cz = z_ref[...].astype(F32) * gc_ref[...].astype(F32)
    czp = zp_ref[...].astype(F32) * gcp_ref[...].astype(F32)
    czp = jnp.where(i % tiles_per_seq == 0, 0.0, czp)
    ext_ref[0:HALO, :] = czp
    ext_ref[HALO:, :] = cz
    cw = cw_ref[...]
    y = (cw[0:1, :] * ext_ref[HALO - 2:HALO - 2 + tm, :]
         + cw[1:2, :] * ext_ref[HALO - 1:HALO - 1 + tm, :]
         + cw[2:3, :] * cz)
    conv_ref[...] = (gb_ref[...].astype(F32) * y).astype(BF16)
    for c in range(o_ref.shape[1] // chunk):
        cols = slice(c * chunk, (c + 1) * chunk)
        ya_ref[:, cols] = lax.dot_general(attn_ref[...], wap_ref[:, cols], (((0,), (0,)), ((), ())),
                                          preferred_element_type=F32)
        y_attn = ya_ref[:, cols]
        y_conv = jnp.dot(conv_ref[...], wcp_ref[:, cols], preferred_element_type=F32)
        t_attn = jnp.tanh(ga_ref[:, cols].astype(F32))
        t_conv = jnp.tanh(gg_ref[:, cols].astype(F32))
        merged_ref[:, cols] = ((y_attn + y_conv) + (t_attn * y_attn + t_conv * y_conv)).astype(BF16)
    o_ref[...] = x_ref[...] + jnp.dot(merged_ref[...], wmo_ref[...], preferred_element_type=F32)


def _mix(attn_t, proj, x2, conv_w, wap, wcp, wmo, seq, *, tm, chunk):
    n, d = x2.shape
    cwid = wcp.shape[0]
    tps = seq // tm
    zc, gbc, gcc = COL_Z // cwid, COL_GB // cwid, COL_GC // cwid
    gac, ggc = COL_GATE_A // d, COL_GATE_C // d
    rb = tm // HALO
    const = lambda i: (0, 0)
    return pl.pallas_call(
        functools.partial(_mix_kernel, tm=tm, tiles_per_seq=tps, chunk=chunk),
        grid=(n // tm,),
        in_specs=[
            pl.BlockSpec((None, ATTN_WIDTH, tm), lambda i: (i // tps, 0, i % tps)),
            pl.BlockSpec((tm, cwid), lambda i: (i, zc)),
            pl.BlockSpec((tm, cwid), lambda i: (i, gbc)),
            pl.BlockSpec((tm, cwid), lambda i: (i, gcc)),
            pl.BlockSpec((HALO, cwid), lambda i: (jnp.maximum(i * rb - 1, 0), zc)),
            pl.BlockSpec((HALO, cwid), lambda i: (jnp.maximum(i * rb - 1, 0), gcc)),
            pl.BlockSpec((tm, d), lambda i: (i, gac)),
            pl.BlockSpec((tm, d), lambda i: (i, ggc)),
            pl.BlockSpec((tm, d), lambda i: (i, 0)),
            pl.BlockSpec((CONV_K, cwid), const),
            pl.BlockSpec(wap.shape, const),
            pl.BlockSpec(wcp.shape, const),
            pl.BlockSpec(wmo.shape, const),
        ],
        out_specs=pl.BlockSpec((tm, d), lambda i: (i, 0)),
        out_shape=jax.ShapeDtypeStruct((n, d), F32),
        scratch_shapes=[pltpu.VMEM((tm + HALO, cwid), F32), pltpu.VMEM((tm, cwid), BF16),
                        pltpu.VMEM((tm, d), F32), pltpu.VMEM((tm, d), BF16)],
        compiler_params=_params(("parallel",)),
        name="mix",
    )(attn_t, proj, proj, proj, proj, proj, proj, proj, x2, conv_w, wap, wcp, wmo)


def _memkv_kernel(m_ref, g_ref, w_ref, o_ref):
    u = _rms(m_ref[...], g_ref[...]).astype(BF16)
    o_ref[...] = jnp.dot(u, w_ref[...], preferred_element_type=F32).astype(o_ref.dtype)


def _memkv(mem2, g, w, *, tm):
    n, d = mem2.shape
    return pl.pallas_call(
        _memkv_kernel,
        grid=(n // tm,),
        in_specs=[
            pl.BlockSpec((tm, d), lambda i: (i, 0)),
            pl.BlockSpec((1, d), lambda i: (0, 0)),
            pl.BlockSpec(w.shape, lambda i: (0, 0)),
        ],
        out_specs=pl.BlockSpec((tm, w.shape[1]), lambda i: (i, 0)),
        out_shape=jax.ShapeDtypeStruct((n, w.shape[1]), BF16),
        compiler_params=_params(("parallel",)),
        name="memkv",
    )(mem2, g, w)


def _xattn_kernel(h_ref, g_ref, wq_ref, kv_ref, wo_ref, o_ref):
    h = h_ref[...]
    u = _rms(h, g_ref[...]).astype(BF16)
    q = jnp.dot(u, wq_ref[...], preferred_element_type=F32) * (X_HEAD_DIM ** -0.5)
    q = q.astype(BF16)
    xw = X_HEADS * X_HEAD_DIM
    outs = []
    for hd in range(X_HEADS):
        cols = slice(hd * X_HEAD_DIM, (hd + 1) * X_HEAD_DIM)
        kh = kv_ref[:, cols]
        vh = kv_ref[:, xw + hd * X_HEAD_DIM:xw + (hd + 1) * X_HEAD_DIM]
        s = lax.dot_general(q[:, cols], kh, (((1,), (1,)), ((), ())), preferred_element_type=F32)
        m = jnp.max(s, axis=-1, keepdims=True)
        p = jnp.exp(s - m)
        denom = jnp.sum(p, axis=-1, keepdims=True)
        outs.append((jnp.dot(p.astype(BF16), vh, preferred_element_type=F32) / denom).astype(BF16))
    o = jnp.concatenate(outs, axis=1)
    o_ref[...] = h + jnp.dot(o, wo_ref[...], preferred_element_type=F32)


def _xattn(h3, g, wq, kv3, wo, *, tm):
    b, t, d = h3.shape
    m, kvw = kv3.shape[1], kv3.shape[2]
    return pl.pallas_call(
        _xattn_kernel,
        grid=(b, t // tm),
        in_specs=[
            pl.BlockSpec((None, tm, d), lambda bi, i: (bi, i, 0)),
            pl.BlockSpec((1, d), lambda bi, i: (0, 0)),
            pl.BlockSpec(wq.shape, lambda bi, i: (0, 0)),
            pl.BlockSpec((None, m, kvw), lambda bi, i: (bi, 0, 0)),
            pl.BlockSpec(wo.shape, lambda bi, i: (0, 0)),
        ],
        out_specs=pl.BlockSpec((None, tm, d), lambda bi, i: (bi, i, 0)),
        out_shape=jax.ShapeDtypeStruct((b, t, d), F32),
        compiler_params=_params(("parallel", "parallel")),
        name="xattn",
    )(h3, g, wq, kv3, wo)


def _ffn_kernel(h_ref, g_ref, wa_ref, wb_ref, wo_ref, gf_ref, o_ref, u_ref, act_ref, *, sub, chunk):
    k = pl.program_id(1)

    @pl.when(k == 0)
    def _():
        h = h_ref[...]
        u_ref[...] = _rms(h, g_ref[...]).astype(BF16)
        o_ref[...] = h

    th = act_ref.shape[1]
    for s in range(th // sub):
        hid = slice(s * sub, (s + 1) * sub)
        a = jnp.dot(u_ref[...], wa_ref[:, hid], preferred_element_type=F32)
        b = jnp.dot(u_ref[...], wb_ref[:, hid], preferred_element_type=F32)
        act_ref[:, hid] = (a * (1.0 + jnp.tanh(a)) * b).astype(BF16)
    for c in range(o_ref.shape[1] // chunk):
        cols = slice(c * chunk, (c + 1) * chunk)
        o_ref[:, cols] += jnp.dot(act_ref[...], wo_ref[:, cols], preferred_element_type=F32)

    @pl.when(k == pl.num_programs(1) - 1)
    def _():
        o_ref[...] = _rms(o_ref[...], gf_ref[...])


def _ffn(h2, g, w_in, w_out, g_final, *, tm, th, sub, chunk):
    n, d = h2.shape
    hidden = w_out.shape[0]
    nk = hidden // th
    return pl.pallas_call(
        functools.partial(_ffn_kernel, sub=sub, chunk=chunk),
        grid=(n // tm, nk),
        in_specs=[
            pl.BlockSpec((tm, d), lambda i, k: (i, 0)),
            pl.BlockSpec((1, d), lambda i, k: (0, 0)),
            pl.BlockSpec((d, th), lambda i, k: (0, k)),
            pl.BlockSpec((d, th), lambda i, k: (0, nk + k)),
            pl.BlockSpec((th, d), lambda i, k: (k, 0)),
            pl.BlockSpec((1, d), lambda i, k: (0, 0)),
        ],
        out_specs=pl.BlockSpec((tm, d), lambda i, k: (i, 0)),
        out_shape=jax.ShapeDtypeStruct((n, d), F32),
        scratch_shapes=[pltpu.VMEM((tm, d), BF16), pltpu.VMEM((tm, th), BF16)],
        compiler_params=_params(("parallel", "arbitrary")),
        name="ffn",
    )(h2, g, w_in, w_in, w_out, g_final)


def _rope_tables(seq):
    half = HEAD_DIM // 2
    inv_freq = ROPE_THETA ** (-jnp.arange(half, dtype=F32) / half)
    ang = jnp.arange(seq, dtype=jnp.int32).astype(F32)[:, None] * inv_freq[None, :]
    cos, sin = jnp.cos(ang), jnp.sin(ang)
    reps = LANES // HEAD_DIM
    cos_t = jnp.tile(jnp.concatenate([cos, cos], axis=1), (1, reps))
    sin_t = jnp.tile(jnp.concatenate([-sin, sin], axis=1), (1, reps))
    return cos_t, sin_t


def _cast_w_in(w):
    col_scale = jnp.where(jnp.arange(w.shape[1]) >= REF_GATE_A, 0.5, 1.0).astype(F32)
    return (w * col_scale).astype(BF16)


def _col_scale(cols, value, lo=0, hi=None):
    idx = jnp.arange(cols)
    hi = cols if hi is None else hi
    return jnp.where((idx >= lo) & (idx < hi), value, 1.0).astype(F32)[None, :]


def kernel(x, mem, g_mix, w_in, conv_w, attn_sinks, w_attn_proj, w_conv_proj, w_mix_out, g_xattn, g_mem,
           w_xq, w_xkv, w_xo, g_ffn, w_ffn_in, w_ffn_out, g_final):
    b, t, d = x.shape
    depth = w_in.shape[0]
    cos_t, sin_t = _rope_tables(t)
    h = x.reshape(b * t, d)
    mem2 = mem.reshape(b * mem.shape[1], d)
    for l in range(depth):
        w_in_bf = _cast_w_in(w_in[l])
        u, kv = _norm_kv(h, g_mix[l][None, :], w_in_bf, cos_t, sin_t, t, tm=1024)
        q_scale = HEAD_DIM ** -0.5 * LOG2E
        hidden = w_ffn_out.shape[1]
        side = [
            (w_ffn_in[l], _col_scale(2 * hidden, 0.5, 0, hidden), 32),
            (w_ffn_out[l], None, 128),
            (w_mix_out[l], _col_scale(d, 0.5), 32),
            (w_attn_proj[l], None, 16),
            (w_conv_proj[l], None, 16),
            (w_xq[l], None, 32),
            (w_xkv[l], None, 32),
            (w_xo[l], None, 16),
        ]
        proj, side_bf = _inproj(u, w_in_bf, cos_t * q_scale, sin_t * q_scale, t, side,
                                tm=1024, tn=2048, chunk=512)
        (w_ffn_in_bf, w_ffn_out_bf, w_mix_out_bf, w_attn_proj_bf, w_conv_proj_bf,
         w_xq_bf, w_xkv_bf, w_xo_bf) = side_bf
        attn = _swa(proj.reshape(b, t, -1), kv.reshape(b, t, -1), attn_sinks[l], tq=1024)
        h = _mix(attn, proj, h, conv_w[l], w_attn_proj_bf, w_conv_proj_bf, w_mix_out_bf, t, tm=256, chunk=1024)
        mkv = _memkv(mem2, g_mem[l][None, :], w_xkv_bf, tm=256)
        h = _xattn(h.reshape(b, t, d), g_xattn[l][None, :], w_xq_bf,
                   mkv.reshape(b, mem.shape[1], -1), w_xo_bf, tm=1024).reshape(b * t, d)
        last = l == depth - 1
        assert last, "fused final norm expects a single layer"
        h = _ffn(h, g_ffn[l][None, :], w_ffn_in_bf, w_ffn_out_bf,
                 g_final[None, :], tm=1024, th=512, sub=256, chunk=512)
    return h.reshape(b, t, d)
```

```python
import functools

import jax
import jax.numpy as jnp
import numpy as np
from jax import lax
from jax.experimental import pallas as pl
from jax.experimental.pallas import tpu as pltpu

F32 = jnp.float32
BF16 = jnp.bfloat16

HEAD_DIM = 64
N_Q_HEADS = 16
N_KV_HEADS = 4
Q_PER_KV = N_Q_HEADS // N_KV_HEADS
ATTN_WIDTH = N_Q_HEADS * HEAD_DIM
KV_WIDTH = N_KV_HEADS * HEAD_DIM
WINDOW = 128
ROPE_THETA = 10000.0
CONV_K = 3
X_HEADS = 4
X_HEAD_DIM = 128
EPS = 1e-6
LOG2E = 1.4426950408889634

D_MODEL = 2048
CONV_WIDTH = 1024
REF_Q = 0
REF_K = REF_Q + ATTN_WIDTH
REF_V = REF_K + KV_WIDTH
REF_Z = REF_V + KV_WIDTH
REF_GB = REF_Z + CONV_WIDTH
REF_GC = REF_GB + CONV_WIDTH
REF_GATE_A = REF_GC + CONV_WIDTH
REF_GATE_C = REF_GATE_A + D_MODEL
COL_GATE_A = 0
COL_GATE_C = COL_GATE_A + D_MODEL
COL_Z = COL_GATE_C + D_MODEL
COL_GB = COL_Z + CONV_WIDTH
COL_GC = COL_GB + CONV_WIDTH
COL_Q = COL_GC + CONV_WIDTH
MAIN_WIDTH = COL_Q + ATTN_WIDTH
_MAIN_SRC = ((REF_GATE_A, D_MODEL), (REF_GATE_C, D_MODEL), (REF_Z, CONV_WIDTH), (REF_GB, CONV_WIDTH),
             (REF_GC, CONV_WIDTH), (REF_Q, ATTN_WIDTH))

LANES = 128
HALO = 16
VMEM_LIMIT = 56 * 1024 * 1024
FFN_VMEM_LIMIT = 60 * 1024 * 1024


def _rms(x, g):
    ms = jnp.mean(x * x, axis=-1, keepdims=True)
    return x * lax.rsqrt(ms + EPS) * g


def _params(sem, vmem_limit=VMEM_LIMIT):
    return pltpu.CompilerParams(dimension_semantics=sem, vmem_limit_bytes=vmem_limit)


def _rope(acc, cos, sin_signed):
    lane = lax.broadcasted_iota(jnp.int32, (1, LANES), 1)
    first_half = (lane % HEAD_DIM) < (HEAD_DIM // 2)
    outs = []
    for c in range(acc.shape[1] // LANES):
        xc = acc[:, c * LANES:(c + 1) * LANES]
        fwd = pltpu.roll(xc, LANES - HEAD_DIM // 2, axis=1)
        bwd = pltpu.roll(xc, HEAD_DIM // 2, axis=1)
        outs.append(xc * cos + jnp.where(first_half, fwd, bwd) * sin_signed)
    return jnp.concatenate(outs, axis=1)


def _norm_kv_kernel(x_ref, g_ref, w_ref, cos_ref, sin_ref, u_ref, kv_ref):
    u = _rms(x_ref[...], g_ref[...]).astype(BF16)
    u_ref[...] = u
    k = jnp.dot(u, w_ref[:, :KV_WIDTH], preferred_element_type=F32)
    kv_ref[:, :KV_WIDTH] = _rope(k, cos_ref[...], sin_ref[...]).astype(kv_ref.dtype)
    kv_ref[:, KV_WIDTH:] = jnp.dot(u, w_ref[:, KV_WIDTH:], preferred_element_type=F32).astype(kv_ref.dtype)


def _norm_kv(x2, g, w_in, cos_t, sin_t, seq, *, tm):
    n, d = x2.shape
    tpb = seq // tm
    assert REF_V == REF_K + KV_WIDTH and REF_K % (2 * KV_WIDTH) == 0
    return pl.pallas_call(
        _norm_kv_kernel,
        grid=(n // tm,),
        in_specs=[
            pl.BlockSpec((tm, d), lambda i: (i, 0)),
            pl.BlockSpec((1, d), lambda i: (0, 0)),
            pl.BlockSpec((d, 2 * KV_WIDTH), lambda i: (0, REF_K // (2 * KV_WIDTH))),
            pl.BlockSpec((tm, LANES), lambda i: (i % tpb, 0)),
            pl.BlockSpec((tm, LANES), lambda i: (i % tpb, 0)),
        ],
        out_specs=[
            pl.BlockSpec((tm, d), lambda i: (i, 0)),
            pl.BlockSpec((tm, 2 * KV_WIDTH), lambda i: (i, 0)),
        ],
        out_shape=[
            jax.ShapeDtypeStruct((n, d), BF16),
            jax.ShapeDtypeStruct((n, 2 * KV_WIDTH), BF16),
        ],
        compiler_params=_params(("parallel",)),
        name="norm_kv",
    )(x2, g, w_in, cos_t, sin_t)


def _inproj_kernel(u_ref, *refs, tn, chunk, side_scaled):
    n_chunks = tn // chunk
    w_refs, refs = refs[:n_chunks], refs[n_chunks:]
    (cos_ref, sin_ref), refs = refs[:2], refs[2:]
    n_side_in = len(side_scaled) + sum(side_scaled)
    side_in, (o_ref, *side_out) = list(refs[:n_side_in]), refs[n_side_in:]
    j = pl.program_id(0)
    for out_ref, scaled in zip(side_out, side_scaled):
        w = side_in.pop(0)[...]
        if scaled:
            w = w * side_in.pop(0)[...]
        out_ref[...] = w.astype(out_ref.dtype)
    n_tiles = MAIN_WIDTH // tn
    q_tile, q_lo = COL_Q // tn, COL_Q % tn
    assert q_lo % chunk == 0 and MAIN_WIDTH == (q_tile + 1) * tn

    def tile(rope_from):
        for c in range(n_chunks):
            acc = jnp.dot(u_ref[...], w_refs[c][...], preferred_element_type=F32)
            if c * chunk >= rope_from:
                acc = _rope(acc, cos_ref[...], sin_ref[...])
            o_ref[:, c * chunk:(c + 1) * chunk] = acc.astype(o_ref.dtype)

    pl.when(j < n_tiles - 1)(lambda: tile(tn))
    pl.when(j == n_tiles - 1)(lambda: tile(q_lo))


def _inproj(u, w_in, cos_t, sin_t, seq, side, *, tm, tn, chunk):
    n, d = u.shape
    tpb = seq // tm
    n_tiles, n_chunks = MAIN_WIDTH // tn, tn // chunk
    n_rows = n // tm

    side_in_specs, side_out_specs, side_out_shapes, side_args = [], [], [], []
    for w, scale, rows in side:
        n_blocks = w.shape[0] // rows
        assert w.shape[0] % rows == 0 and rows % HALO == 0 and n_blocks <= n_tiles * n_rows
        index = lambda j, i, n_blocks=n_blocks: (jnp.minimum(j * n_rows + i, n_blocks - 1), 0)
        side_in_specs.append(pl.BlockSpec((rows, w.shape[1]), index))
        side_args.append(w)
        if scale is not None:
            side_in_specs.append(pl.BlockSpec((1, w.shape[1]), lambda j, i: (0, 0)))
            side_args.append(scale)
        side_out_specs.append(pl.BlockSpec((rows, w.shape[1]), index))
        side_out_shapes.append(jax.ShapeDtypeStruct(w.shape, BF16))
    src_blocks = []
    for ref_off, width in _MAIN_SRC:
        assert ref_off % chunk == 0 and width % chunk == 0
        src_blocks += [ref_off // chunk + b for b in range(width // chunk)]
    assert len(src_blocks) == n_tiles * n_chunks

    def w_spec(c):
        def index(j, i):
            blk = 0
            for jj in range(n_tiles):
                blk = blk + jnp.where(j == jj, src_blocks[jj * n_chunks + c], 0)
            return (0, blk)
        return pl.BlockSpec((d, chunk), index)

    proj, *side_bf = pl.pallas_call(
        functools.partial(_inproj_kernel, tn=tn, chunk=chunk,
                          side_scaled=tuple(scale is not None for _, scale, _ in side)),
        grid=(n_tiles, n_rows),
        in_specs=[pl.BlockSpec((tm, d), lambda j, i: (i, 0))]
        + [w_spec(c) for c in range(n_chunks)]
        + [pl.BlockSpec((tm, LANES), lambda j, i: (i % tpb, 0)),
           pl.BlockSpec((tm, LANES), lambda j, i: (i % tpb, 0))]
        + side_in_specs,
        out_specs=[pl.BlockSpec((tm, tn), lambda j, i: (i, j))] + side_out_specs,
        out_shape=[jax.ShapeDtypeStruct((n, MAIN_WIDTH), BF16)] + side_out_shapes,
        compiler_params=_params(("arbitrary", "arbitrary")),
        name="inproj",
    )(u, *([w_in] * n_chunks), cos_t, sin_t, *side_args)
    return proj, side_bf


def _swa_kernel(sink_ref, q_ref, k_ref, v_ref, kp_ref, vp_ref, o_ref, bias_ref, st_ref, *, nblk):
    i = pl.program_id(1)
    stacked = Q_PER_KV * WINDOW
    key = lax.broadcasted_iota(jnp.int32, (2 * WINDOW, stacked), 0)
    qry = lax.broadcasted_iota(jnp.int32, (2 * WINDOW, stacked), 1) % WINDOW
    band = (key > qry) & (key <= qry + WINDOW)
    bias_ref[0] = jnp.where(band & ((key >= WINDOW) | (i > 0)), 0.0, -jnp.inf)
    bias_ref[1] = jnp.where(band, 0.0, -jnp.inf)
    lane_head = lax.broadcasted_iota(jnp.int32, (1, stacked), 1) // WINDOW

    def band_of(ref, prev_ref, b, g):
        cols = slice(g * HEAD_DIM, (g + 1) * HEAD_DIM)
        if b == 0:
            return jnp.concatenate([prev_ref[:, cols], ref[0:WINDOW, cols]], axis=0)
        return ref[(b - 1) * WINDOW:(b + 1) * WINDOW, cols]

    def scores(b, g):
        rows = slice(b * WINDOW, (b + 1) * WINDOW)
        heads = range(g * Q_PER_KV, (g + 1) * Q_PER_KV)
        qs = jnp.concatenate([q_ref[rows, h * HEAD_DIM:(h + 1) * HEAD_DIM] for h in heads], axis=0)
        kh = band_of(k_ref, kp_ref, b, g)
        return lax.dot_general(kh, qs, (((1,), (1,)), ((), ())), preferred_element_type=F32)

    def finish(b, g, st):
        rows = slice(b * WINDOW, (b + 1) * WINDOW)
        heads = range(g * Q_PER_KV, (g + 1) * Q_PER_KV)
        sink = jnp.zeros((1, stacked), F32)
        for n, h in enumerate(heads):
            sink = jnp.where(lane_head == n, sink_ref[h] * LOG2E, sink)
        st = st + bias_ref[0 if b == 0 else 1]
        m = jnp.maximum(jnp.max(st, axis=0, keepdims=True), sink)
        p = jnp.exp2(st - m)
        denom = jnp.sum(p, axis=0, keepdims=True) + jnp.exp2(sink - m)
        vh = band_of(v_ref, vp_ref, b, g)
        ot = lax.dot_general(vh, p.astype(BF16), (((0,), (0,)), ((), ())), preferred_element_type=F32)
        ot = ot * (1.0 / denom)
        for n, h in enumerate(heads):
            o_ref[h * HEAD_DIM:(h + 1) * HEAD_DIM, rows] = ot[:, n * WINDOW:(n + 1) * WINDOW].astype(o_ref.dtype)

    pairs = [(b, g) for b in range(nblk) for g in range(N_KV_HEADS)]
    ahead, slots = 2, 3
    for n in range(ahead):
        st_ref[n] = scores(*pairs[n])
    for n, (b, g) in enumerate(pairs):
        if n + ahead < len(pairs):
            st_ref[(n + ahead) % slots] = scores(*pairs[n + ahead])
        finish(b, g, st_ref[n % slots])


def _swa(proj3, kv3, sinks, *, tq):
    b, t, _ = proj3.shape
    nblk = tq // WINDOW
    qcol = COL_Q // ATTN_WIDTH
    prev = lambda bi, i: jnp.maximum(i * nblk - 1, 0)
    return pl.pallas_call(
        functools.partial(_swa_kernel, nblk=nblk),
        grid=(b, t // tq),
        in_specs=[
            pl.BlockSpec(memory_space=pltpu.SMEM),
            pl.BlockSpec((None, tq, ATTN_WIDTH), lambda bi, i: (bi, i, qcol)),
            pl.BlockSpec((None, tq, KV_WIDTH), lambda bi, i: (bi, i, 0)),
            pl.BlockSpec((None, tq, KV_WIDTH), lambda bi, i: (bi, i, 1)),
            pl.BlockSpec((None, WINDOW, KV_WIDTH), lambda bi, i: (bi, prev(bi, i), 0)),
            pl.BlockSpec((None, WINDOW, KV_WIDTH), lambda bi, i: (bi, prev(bi, i), 1)),
        ],
        out_specs=pl.BlockSpec((None, ATTN_WIDTH, tq), lambda bi, i: (bi, 0, i)),
        out_shape=jax.ShapeDtypeStruct((b, ATTN_WIDTH, t), BF16),
        scratch_shapes=[pltpu.VMEM((2, 2 * WINDOW, Q_PER_KV * WINDOW), F32),
                        pltpu.VMEM((3, 2 * WINDOW, Q_PER_KV * WINDOW), F32)],
        compiler_params=_params(("parallel", "arbitrary")),
        name="swa",
    )(sinks, proj3, kv3, kv3, kv3, kv3)


def _mix_kernel(attn_ref, z_ref, gb_ref, gc_ref, zp_ref, gcp_ref, ga_ref, gg_ref, x_ref, cw_ref,
                wap_ref, wcp_ref, wmo_ref, o_ref, ext_ref, conv_ref, ya_ref, merged_ref,
                *, tm, tiles_per_seq, chunk):
    i = pl.program_id(0)
    cz = z_ref[...].astype(F32) * gc_ref[...].astype(F32)
    czp = zp_ref[...].astype(F32) * gcp_ref[...].astype(F32)
    czp = jnp.where(i % tiles_per_seq == 0, 0.0, czp)
    ext_ref[0:HALO, :] = czp
    ext_ref[HALO:, :] = cz
    cw = cw_ref[...]
    y = (cw[0:1, :] * ext_ref[HALO - 2:HALO - 2 + tm, :]
         + cw[1:2, :] * ext_ref[HALO - 1:HALO - 1 + tm, :]
         + cw[2:3, :] * cz)
    conv_ref[...] = (gb_ref[...].astype(F32) * y).astype(BF16)
    for c in range(o_ref.shape[1] // chunk):
        cols = slice(c * chunk, (c + 1) * chunk)
        ya_ref[:, cols] = lax.dot_general(attn_ref[...], wap_ref[:, cols], (((0,), (0,)), ((), ())),
                                          preferred_element_type=F32)
        y_attn = ya_ref[:, cols]
        y_conv = jnp.dot(conv_ref[...], wcp_ref[:, cols], preferred_element_type=F32)
        t_attn = jnp.tanh(ga_ref[:, cols].astype(F32))
        t_conv = jnp.tanh(gg_ref[:, cols].astype(F32))
        merged_ref[:, cols] = ((y_attn + y_conv) + (t_attn * y_attn + t_conv * y_conv)).astype(BF16)
    o_ref[...] = x_ref[...] + jnp.dot(merged_ref[...], wmo_ref[...], preferred_element_type=F32)


def _mix(attn_t, proj, x2, conv_w, wap, wcp, wmo, seq, *, tm, chunk):
    n, d = x2.shape
    cwid = wcp.shape[0]
    tps = seq // tm
    zc, gbc, gcc = COL_Z // cwid, COL_GB // cwid, COL_GC // cwid
    gac, ggc = COL_GATE_A // d, COL_GATE_C // d
    rb = tm // HALO
    const = lambda i: (0, 0)
    return pl.pallas_call(
        functools.partial(_mix_kernel, tm=tm, tiles_per_seq=tps, chunk=chunk),
        grid=(n // tm,),
        in_specs=[
            pl.BlockSpec((None, ATTN_WIDTH, tm), lambda i: (i // tps, 0, i % tps)),
            pl.BlockSpec((tm, cwid), lambda i: (i, zc)),
            pl.BlockSpec((tm, cwid), lambda i: (i, gbc)),
            pl.BlockSpec((tm, cwid), lambda i: (i, gcc)),
            pl.BlockSpec((HALO, cwid), lambda i: (jnp.maximum(i * rb - 1, 0), zc)),
            pl.BlockSpec((HALO, cwid), lambda i: (jnp.maximum(i * rb - 1, 0), gcc)),
            pl.BlockSpec((tm, d), lambda i: (i, gac)),
            pl.BlockSpec((tm, d), lambda i: (i, ggc)),
            pl.BlockSpec((tm, d), lambda i: (i, 0)),
            pl.BlockSpec((CONV_K, cwid), const),
            pl.BlockSpec(wap.shape, const),
            pl.BlockSpec(wcp.shape, const),
            pl.BlockSpec(wmo.shape, const),
        ],
        out_specs=pl.BlockSpec((tm, d), lambda i: (i, 0)),
        out_shape=jax.ShapeDtypeStruct((n, d), F32),
        scratch_shapes=[pltpu.VMEM((tm + HALO, cwid), F32), pltpu.VMEM((tm, cwid), BF16),
                        pltpu.VMEM((tm, d), F32), pltpu.VMEM((tm, d), BF16)],
        compiler_params=_params(("parallel",)),
        name="mix",
    )(attn_t, proj, proj, proj, proj, proj, proj, proj, x2, conv_w, wap, wcp, wmo)


def _memkv_kernel(m_ref, g_ref, w_ref, o_ref):
    u = _rms(m_ref[...], g_ref[...]).astype(BF16)
    o_ref[...] = jnp.dot(u, w_ref[...], preferred_element_type=F32).astype(o_ref.dtype)


def _memkv(mem2, g, w, *, tm):
    n, d = mem2.shape
    return pl.pallas_call(
        _memkv_kernel,
        grid=(n // tm,),
        in_specs=[
            pl.BlockSpec((tm, d), lambda i: (i, 0)),
            pl.BlockSpec((1, d), lambda i: (0, 0)),
            pl.BlockSpec(w.shape, lambda i: (0, 0)),
        ],
        out_specs=pl.BlockSpec((tm, w.shape[1]), lambda i: (i, 0)),
        out_shape=jax.ShapeDtypeStruct((n, w.shape[1]), BF16),
        compiler_params=_params(("parallel",)),
        name="memkv",
    )(mem2, g, w)


def _xattn_kernel(h_ref, g_ref, wq_ref, kv_ref, wo_ref, o_ref):
    h = h_ref[...]
    u = _rms(h, g_ref[...]).astype(BF16)
    q = jnp.dot(u, wq_ref[...], preferred_element_type=F32) * (X_HEAD_DIM ** -0.5)
    q = q.astype(BF16)
    xw = X_HEADS * X_HEAD_DIM
    outs = []
    for hd in range(X_HEADS):
        cols = slice(hd * X_HEAD_DIM, (hd + 1) * X_HEAD_DIM)
        kh = kv_ref[:, cols]
        vh = kv_ref[:, xw + hd * X_HEAD_DIM:xw + (hd + 1) * X_HEAD_DIM]
        s = lax.dot_general(q[:, cols], kh, (((1,), (1,)), ((), ())), preferred_element_type=F32)
        m = jnp.max(s, axis=-1, keepdims=True)
        p = jnp.exp(s - m)
        denom = jnp.sum(p, axis=-1, keepdims=True)
        outs.append((jnp.dot(p.astype(BF16), vh, preferred_element_type=F32) / denom).astype(BF16))
    o = jnp.concatenate(outs, axis=1)
    o_ref[...] = h + jnp.dot(o, wo_ref[...], preferred_element_type=F32)


def _xattn(h3, g, wq, kv3, wo, *, tm):
    b, t, d = h3.shape
    m, kvw = kv3.shape[1], kv3.shape[2]
    return pl.pallas_call(
        _xattn_kernel,
        grid=(b, t // tm),
        in_specs=[
            pl.BlockSpec((None, tm, d), lambda bi, i: (bi, i, 0)),
            pl.BlockSpec((1, d), lambda bi, i: (0, 0)),
            pl.BlockSpec(wq.shape, lambda bi, i: (0, 0)),
            pl.BlockSpec((None, m, kvw), lambda bi, i: (bi, 0, 0)),
            pl.BlockSpec(wo.shape, lambda bi, i: (0, 0)),
        ],
        out_specs=pl.BlockSpec((None, tm, d), lambda bi, i: (bi, i, 0)),
        out_shape=jax.ShapeDtypeStruct((b, t, d), F32),
        compiler_params=_params(("parallel", "parallel")),
        name="xattn",
    )(h3, g, wq, kv3, wo)


def _ffn_kernel(h_ref, g_ref, wa_ref, wb_ref, wo_ref, gf_ref, o_ref, u_ref, act_ref, *, sub, chunk):
    k = pl.program_id(1)

    last = pl.num_programs(1) - 1
    d = o_ref.shape[1]

    def hidden_tile(u, final_norm):
        th = act_ref.shape[1]
        for s in range(th // sub):
            hid = slice(s * sub, (s + 1) * sub)
            a = jnp.dot(u, wa_ref[:, hid], preferred_element_type=F32)
            b = jnp.dot(u, wb_ref[:, hid], preferred_element_type=F32)
            act_ref[:, hid] = (a * (1.0 + jnp.tanh(a)) * b).astype(BF16)
        sum_sq = jnp.zeros((o_ref.shape[0], 1), F32)
        for c in range(d // chunk):
            cols = slice(c * chunk, (c + 1) * chunk)
            y = o_ref[:, cols] + jnp.dot(act_ref[...], wo_ref[:, cols], preferred_element_type=F32)
            o_ref[:, cols] = y
            if final_norm:
                sum_sq = sum_sq + jnp.sum(y * y, axis=-1, keepdims=True)
        if final_norm:
            inv_rms = lax.rsqrt(sum_sq * (1.0 / d) + EPS)
            for c in range(d // chunk):
                cols = slice(c * chunk, (c + 1) * chunk)
                o_ref[:, cols] = o_ref[:, cols] * inv_rms * gf_ref[:, cols]

    @pl.when(k == 0)
    def _():
        h = h_ref[...]
        u = _rms(h, g_ref[...]).astype(BF16)
        u_ref[...] = u
        o_ref[...] = h
        hidden_tile(u, False)

    @pl.when((k > 0) & (k < last))
    def _():
        hidden_tile(u_ref[...], False)

    @pl.when(k == last)
    def _():
        hidden_tile(u_ref[...], True)


def _ffn(h2, g, w_in, w_out, g_final, *, tm, th, sub, chunk):
    n, d = h2.shape
    hidden = w_out.shape[0]
    nk = hidden // th
    assert nk >= 2
    return pl.pallas_call(
        functools.partial(_ffn_kernel, sub=sub, chunk=chunk),
        grid=(n // tm, nk),
        in_specs=[
            pl.BlockSpec((tm, d), lambda i, k: (i, 0)),
            pl.BlockSpec((1, d), lambda i, k: (0, 0)),
            pl.BlockSpec((d, th), lambda i, k: (0, k)),
            pl.BlockSpec((d, th), lambda i, k: (0, nk + k)),
            pl.BlockSpec((th, d), lambda i, k: (k, 0)),
            pl.BlockSpec((1, d), lambda i, k: (0, 0)),
        ],
        out_specs=pl.BlockSpec((tm, d), lambda i, k: (i, 0)),
        out_shape=jax.ShapeDtypeStruct((n, d), F32),
        scratch_shapes=[pltpu.VMEM((tm, d), BF16), pltpu.VMEM((tm, th), BF16)],
        compiler_params=_params(("parallel", "arbitrary"), FFN_VMEM_LIMIT),
        name="ffn",
    )(h2, g, w_in, w_in, w_out, g_final)


def _rope_tables(seq):
    half = HEAD_DIM // 2
    inv_freq = ROPE_THETA ** (-jnp.arange(half, dtype=F32) / half)
    ang = jnp.arange(seq, dtype=jnp.int32).astype(F32)[:, None] * inv_freq[None, :]
    cos, sin = jnp.cos(ang), jnp.sin(ang)
    reps = LANES // HEAD_DIM
    cos_t = jnp.tile(jnp.concatenate([cos, cos], axis=1), (1, reps))
    sin_t = jnp.tile(jnp.concatenate([-sin, sin], axis=1), (1, reps))
    return cos_t, sin_t


def _cast_w_in(w):
    col_scale = jnp.where(jnp.arange(w.shape[1]) >= REF_GATE_A, 0.5, 1.0).astype(F32)
    return (w * col_scale).astype(BF16)


def _col_scale(cols, value, lo=0, hi=None):
    idx = jnp.arange(cols)
    hi = cols if hi is None else hi
    return jnp.where((idx >= lo) & (idx < hi), value, 1.0).astype(F32)[None, :]


def kernel(x, mem, g_mix, w_in, conv_w, attn_sinks, w_attn_proj, w_conv_proj, w_mix_out, g_xattn, g_mem,
           w_xq, w_xkv, w_xo, g_ffn, w_ffn_in, w_ffn_out, g_final):
    b, t, d = x.shape
    depth = w_in.shape[0]
    cos_t, sin_t = _rope_tables(t)
    h = x.reshape(b * t, d)
    mem2 = mem.reshape(b * mem.shape[1], d)
    for l in range(depth):
        w_in_bf = _cast_w_in(w_in[l])
        u, kv = _norm_kv(h, g_mix[l][None, :], w_in_bf, cos_t, sin_t, t, tm=1024)
        q_scale = HEAD_DIM ** -0.5 * LOG2E
        hidden = w_ffn_out.shape[1]
        side = [
            (w_ffn_in[l], _col_scale(2 * hidden, 0.5, 0, hidden), 32),
            (w_ffn_out[l], None, 128),
            (w_mix_out[l], _col_scale(d, 0.5), 32),
            (w_attn_proj[l], None, 16),
            (w_conv_proj[l], None, 16),
            (w_xq[l], None, 32),
            (w_xkv[l], None, 32),
            (w_xo[l], None, 16),
        ]
        proj, side_bf = _inproj(u, w_in_bf, cos_t * q_scale, sin_t * q_scale, t, side,
                                tm=1024, tn=2048, chunk=512)
        (w_ffn_in_bf, w_ffn_out_bf, w_mix_out_bf, w_attn_proj_bf, w_conv_proj_bf,
         w_xq_bf, w_xkv_bf, w_xo_bf) = side_bf
        attn = _swa(proj.reshape(b, t, -1), kv.reshape(b, t, -1), attn_sinks[l], tq=1024)
        h = _mix(attn, proj, h, conv_w[l], w_attn_proj_bf, w_conv_proj_bf, w_mix_out_bf, t, tm=256, chunk=1024)
        mkv = _memkv(mem2, g_mem[l][None, :], w_xkv_bf, tm=256)
        h = _xattn(h.reshape(b, t, d), g_xattn[l][None, :], w_xq_bf,
                   mkv.reshape(b, mem.shape[1], -1), w_xo_bf, tm=1024).reshape(b * t, d)
        last = l == depth - 1
        assert last, "fused final norm expects a single layer"
        h = _ffn(h, g_ffn[l][None, :], w_ffn_in_bf, w_ffn_out_bf,
                 g_final[None, :], tm=1024, th=512, sub=256, chunk=512)
    return h.reshape(b, t, d)
```

```python
import functools

import jax
import jax.numpy as jnp
import numpy as np
from jax import lax
from jax.experimental import pallas as pl
from jax.experimental.pallas import tpu as pltpu

F32 = jnp.float32
BF16 = jnp.bfloat16

HEAD_DIM = 64
N_Q_HEADS = 16
N_KV_HEADS = 4
Q_PER_KV = N_Q_HEADS // N_KV_HEADS
ATTN_WIDTH = N_Q_HEADS * HEAD_DIM
KV_WIDTH = N_KV_HEADS * HEAD_DIM
WINDOW = 128
ROPE_THETA = 10000.0
CONV_K = 3
X_HEADS = 4
X_HEAD_DIM = 128
EPS = 1e-6
LOG2E = 1.4426950408889634

D_MODEL = 2048
CONV_WIDTH = 1024
REF_Q = 0
REF_K = REF_Q + ATTN_WIDTH
REF_V = REF_K + KV_WIDTH
REF_Z = REF_V + KV_WIDTH
REF_GB = REF_Z + CONV_WIDTH
REF_GC = REF_GB + CONV_WIDTH
REF_GATE_A = REF_GC + CONV_WIDTH
REF_GATE_C = REF_GATE_A + D_MODEL
COL_GATE_C = 0
COL_Z = COL_GATE_C + D_MODEL
COL_GB = COL_Z + CONV_WIDTH
COL_GC = COL_GB + CONV_WIDTH
COL_Q = COL_GC + CONV_WIDTH
MAIN_WIDTH = COL_Q + ATTN_WIDTH
_MAIN_SRC = ((REF_GATE_C, D_MODEL), (REF_Z, CONV_WIDTH), (REF_GB, CONV_WIDTH),
             (REF_GC, CONV_WIDTH), (REF_Q, ATTN_WIDTH))

LANES = 128
HALO = 16
VMEM_LIMIT = 56 * 1024 * 1024
FFN_VMEM_LIMIT = 60 * 1024 * 1024


def _rms(x, g):
    ms = jnp.mean(x * x, axis=-1, keepdims=True)
    return x * lax.rsqrt(ms + EPS) * g


def _params(sem, vmem_limit=VMEM_LIMIT):
    return pltpu.CompilerParams(dimension_semantics=sem, vmem_limit_bytes=vmem_limit)


def _rope(acc, cos, sin_signed):
    lane = lax.broadcasted_iota(jnp.int32, (1, LANES), 1)
    first_half = (lane % HEAD_DIM) < (HEAD_DIM // 2)
    outs = []
    for c in range(acc.shape[1] // LANES):
        xc = acc[:, c * LANES:(c + 1) * LANES]
        fwd = pltpu.roll(xc, LANES - HEAD_DIM // 2, axis=1)
        bwd = pltpu.roll(xc, HEAD_DIM // 2, axis=1)
        outs.append(xc * cos + jnp.where(first_half, fwd, bwd) * sin_signed)
    return jnp.concatenate(outs, axis=1)


def _norm_proj_kernel(x_ref, g_ref, wkv_ref, *refs, chunk):
    n_chunks = D_MODEL // chunk
    wga_refs, (cos_ref, sin_ref, u_ref, kv_ref, ga_ref) = refs[:n_chunks], refs[n_chunks:]
    u = _rms(x_ref[...], g_ref[...]).astype(BF16)
    u_ref[...] = u
    k = jnp.dot(u, wkv_ref[:, :KV_WIDTH], preferred_element_type=F32)
    kv_ref[:, :KV_WIDTH] = _rope(k, cos_ref[...], sin_ref[...]).astype(kv_ref.dtype)
    kv_ref[:, KV_WIDTH:] = jnp.dot(u, wkv_ref[:, KV_WIDTH:], preferred_element_type=F32).astype(kv_ref.dtype)
    for c in range(n_chunks):
        ga_ref[:, c * chunk:(c + 1) * chunk] = jnp.dot(
            u, wga_refs[c][...], preferred_element_type=F32).astype(ga_ref.dtype)


def _norm_proj(x2, g, w_in, cos_t, sin_t, seq, *, tm, chunk):
    n, d = x2.shape
    tpb = seq // tm
    assert REF_V == REF_K + KV_WIDTH and REF_K % (2 * KV_WIDTH) == 0 and REF_GATE_A % chunk == 0
    ga_specs = [pl.BlockSpec((d, chunk), lambda i, c=c: (0, REF_GATE_A // chunk + c))
                for c in range(D_MODEL // chunk)]
    return pl.pallas_call(
        functools.partial(_norm_proj_kernel, chunk=chunk),
        grid=(n // tm,),
        in_specs=[
            pl.BlockSpec((tm, d), lambda i: (i, 0)),
            pl.BlockSpec((1, d), lambda i: (0, 0)),
            pl.BlockSpec((d, 2 * KV_WIDTH), lambda i: (0, REF_K // (2 * KV_WIDTH))),
        ] + ga_specs + [
            pl.BlockSpec((tm, LANES), lambda i: (i % tpb, 0)),
            pl.BlockSpec((tm, LANES), lambda i: (i % tpb, 0)),
        ],
        out_specs=[
            pl.BlockSpec((tm, d), lambda i: (i, 0)),
            pl.BlockSpec((tm, 2 * KV_WIDTH), lambda i: (i, 0)),
            pl.BlockSpec((tm, D_MODEL), lambda i: (i, 0)),
        ],
        out_shape=[
            jax.ShapeDtypeStruct((n, d), BF16),
            jax.ShapeDtypeStruct((n, 2 * KV_WIDTH), BF16),
            jax.ShapeDtypeStruct((n, D_MODEL), BF16),
        ],
        compiler_params=_params(("parallel",)),
        name="norm_proj",
    )(x2, g, w_in, *([w_in] * (D_MODEL // chunk)), cos_t, sin_t)


def _inproj_kernel(u_ref, *refs, tn, chunk, side_scaled):
    n_chunks = tn // chunk
    w_refs, refs = refs[:n_chunks], refs[n_chunks:]
    (cos_ref, sin_ref), refs = refs[:2], refs[2:]
    n_side_in = len(side_scaled) + sum(side_scaled)
    side_in, (o_ref, *side_out) = list(refs[:n_side_in]), refs[n_side_in:]
    j = pl.program_id(0)
    for out_ref, scaled in zip(side_out, side_scaled):
        w = side_in.pop(0)[...]
        if scaled:
            w = w * side_in.pop(0)[...]
        out_ref[...] = w.astype(out_ref.dtype)
    n_tiles = MAIN_WIDTH // tn
    q_tile, q_lo = COL_Q // tn, COL_Q % tn
    assert q_lo % chunk == 0 and MAIN_WIDTH == (q_tile + 1) * tn

    def tile(rope_from):
        for c in range(n_chunks):
            acc = jnp.dot(u_ref[...], w_refs[c][...], preferred_element_type=F32)
            if c * chunk >= rope_from:
                acc = _rope(acc, cos_ref[...], sin_ref[...])
            o_ref[:, c * chunk:(c + 1) * chunk] = acc.astype(o_ref.dtype)

    pl.when(j < n_tiles - 1)(lambda: tile(tn))
    pl.when(j == n_tiles - 1)(lambda: tile(q_lo))


def _inproj(u, w_in, cos_t, sin_t, seq, side, *, tm, tn, chunk):
    n, d = u.shape
    tpb = seq // tm
    n_tiles, n_chunks = MAIN_WIDTH // tn, tn // chunk
    n_rows = n // tm

    side_in_specs, side_out_specs, side_out_shapes, side_args = [], [], [], []
    for w, scale, rows in side:
        n_blocks = w.shape[0] // rows
        assert w.shape[0] % rows == 0 and rows % HALO == 0 and n_blocks <= n_tiles * n_rows
        index = lambda j, i, n_blocks=n_blocks: (jnp.minimum(j * n_rows + i, n_blocks - 1), 0)
        side_in_specs.append(pl.BlockSpec((rows, w.shape[1]), index))
        side_args.append(w)
        if scale is not None:
            side_in_specs.append(pl.BlockSpec((1, w.shape[1]), lambda j, i: (0, 0)))
            side_args.append(scale)
        side_out_specs.append(pl.BlockSpec((rows, w.shape[1]), index))
        side_out_shapes.append(jax.ShapeDtypeStruct(w.shape, BF16))
    src_blocks = []
    for ref_off, width in _MAIN_SRC:
        assert ref_off % chunk == 0 and width % chunk == 0
        src_blocks += [ref_off // chunk + b for b in range(width // chunk)]
    assert len(src_blocks) == n_tiles * n_chunks

    def w_spec(c):
        def index(j, i):
            blk = 0
            for jj in range(n_tiles):
                blk = blk + jnp.where(j == jj, src_blocks[jj * n_chunks + c], 0)
            return (0, blk)
        return pl.BlockSpec((d, chunk), index)

    proj, *side_bf = pl.pallas_call(
        functools.partial(_inproj_kernel, tn=tn, chunk=chunk,
                          side_scaled=tuple(scale is not None for _, scale, _ in side)),
        grid=(n_tiles, n_rows),
        in_specs=[pl.BlockSpec((tm, d), lambda j, i: (i, 0))]
        + [w_spec(c) for c in range(n_chunks)]
        + [pl.BlockSpec((tm, LANES), lambda j, i: (i % tpb, 0)),
           pl.BlockSpec((tm, LANES), lambda j, i: (i % tpb, 0))]
        + side_in_specs,
        out_specs=[pl.BlockSpec((tm, tn), lambda j, i: (i, j))] + side_out_specs,
        out_shape=[jax.ShapeDtypeStruct((n, MAIN_WIDTH), BF16)] + side_out_shapes,
        compiler_params=_params(("arbitrary", "arbitrary")),
        name="inproj",
    )(u, *([w_in] * n_chunks), cos_t, sin_t, *side_args)
    return proj, side_bf


def _swa_kernel(sink_ref, q_ref, k_ref, v_ref, kp_ref, vp_ref, o_ref, bias_ref, st_ref, *, nblk):
    i = pl.program_id(1)
    stacked = Q_PER_KV * WINDOW
    key = lax.broadcasted_iota(jnp.int32, (2 * WINDOW, stacked), 0)
    qry = lax.broadcasted_iota(jnp.int32, (2 * WINDOW, stacked), 1) % WINDOW
    band = (key > qry) & (key <= qry + WINDOW)
    bias_ref[0] = jnp.where(band & ((key >= WINDOW) | (i > 0)), 0.0, -jnp.inf)
    bias_ref[1] = jnp.where(band, 0.0, -jnp.inf)
    lane_head = lax.broadcasted_iota(jnp.int32, (1, stacked), 1) // WINDOW

    def band_of(ref, prev_ref, b, g):
        cols = slice(g * HEAD_DIM, (g + 1) * HEAD_DIM)
        if b == 0:
            return jnp.concatenate([prev_ref[:, cols], ref[0:WINDOW, cols]], axis=0)
        return ref[(b - 1) * WINDOW:(b + 1) * WINDOW, cols]

    def scores(b, g):
        rows = slice(b * WINDOW, (b + 1) * WINDOW)
        heads = range(g * Q_PER_KV, (g + 1) * Q_PER_KV)
        qs = jnp.concatenate([q_ref[rows, h * HEAD_DIM:(h + 1) * HEAD_DIM] for h in heads], axis=0)
        kh = band_of(k_ref, kp_ref, b, g)
        return lax.dot_general(kh, qs, (((1,), (1,)), ((), ())), preferred_element_type=F32)

    def finish(b, g, st):
        rows = slice(b * WINDOW, (b + 1) * WINDOW)
        heads = range(g * Q_PER_KV, (g + 1) * Q_PER_KV)
        sink = jnp.zeros((1, stacked), F32)
        for n, h in enumerate(heads):
            sink = jnp.where(lane_head == n, sink_ref[h] * LOG2E, sink)
        st = st + bias_ref[0 if b == 0 else 1]
        m = jnp.maximum(jnp.max(st, axis=0, keepdims=True), sink)
        p = jnp.exp2(st - m)
        denom = jnp.sum(p, axis=0, keepdims=True) + jnp.exp2(sink - m)
        vh = band_of(v_ref, vp_ref, b, g)
        ot = lax.dot_general(vh, p.astype(BF16), (((0,), (0,)), ((), ())), preferred_element_type=F32)
        ot = ot * (1.0 / denom)
        for n, h in enumerate(heads):
            o_ref[h * HEAD_DIM:(h + 1) * HEAD_DIM, rows] = ot[:, n * WINDOW:(n + 1) * WINDOW].astype(o_ref.dtype)

    pairs = [(b, g) for b in range(nblk) for g in range(N_KV_HEADS)]
    ahead, slots = 2, 3
    for n in range(ahead):
        st_ref[n] = scores(*pairs[n])
    for n, (b, g) in enumerate(pairs):
        if n + ahead < len(pairs):
            st_ref[(n + ahead) % slots] = scores(*pairs[n + ahead])
        finish(b, g, st_ref[n % slots])


def _swa(proj3, kv3, sinks, *, tq):
    b, t, _ = proj3.shape
    nblk = tq // WINDOW
    qcol = COL_Q // ATTN_WIDTH
    prev = lambda bi, i: jnp.maximum(i * nblk - 1, 0)
    return pl.pallas_call(
        functools.partial(_swa_kernel, nblk=nblk),
        grid=(b, t // tq),
        in_specs=[
            pl.BlockSpec(memory_space=pltpu.SMEM),
            pl.BlockSpec((None, tq, ATTN_WIDTH), lambda bi, i: (bi, i, qcol)),
            pl.BlockSpec((None, tq, KV_WIDTH), lambda bi, i: (bi, i, 0)),
            pl.BlockSpec((None, tq, KV_WIDTH), lambda bi, i: (bi, i, 1)),
            pl.BlockSpec((None, WINDOW, KV_WIDTH), lambda bi, i: (bi, prev(bi, i), 0)),
            pl.BlockSpec((None, WINDOW, KV_WIDTH), lambda bi, i: (bi, prev(bi, i), 1)),
        ],
        out_specs=pl.BlockSpec((None, ATTN_WIDTH, tq), lambda bi, i: (bi, 0, i)),
        out_shape=jax.ShapeDtypeStruct((b, ATTN_WIDTH, t), BF16),
        scratch_shapes=[pltpu.VMEM((2, 2 * WINDOW, Q_PER_KV * WINDOW), F32),
                        pltpu.VMEM((3, 2 * WINDOW, Q_PER_KV * WINDOW), F32)],
        compiler_params=_params(("parallel", "arbitrary")),
        name="swa",
    )(sinks, proj3, kv3, kv3, kv3, kv3)


def _mix_kernel(attn_ref, z_ref, gb_ref, gc_ref, zp_ref, gcp_ref, ga_ref, gg_ref, x_ref, cw_ref,
                wap_ref, wcp_ref, wmo_ref, o_ref, ext_ref, conv_ref, ya_ref, merged_ref,
                *, tm, tiles_per_seq, chunk):
    i = pl.program_id(0)
    cz = z_ref[...].astype(F32) * gc_ref[...].astype(F32)
    czp = zp_ref[...].astype(F32) * gcp_ref[...].astype(F32)
    czp = jnp.where(i % tiles_per_seq == 0, 0.0, czp)
    ext_ref[0:HALO, :] = czp
    ext_ref[HALO:, :] = cz
    cw = cw_ref[...]
    y = (cw[0:1, :] * ext_ref[HALO - 2:HALO - 2 + tm, :]
         + cw[1:2, :] * ext_ref[HALO - 1:HALO - 1 + tm, :]
         + cw[2:3, :] * cz)
    conv_ref[...] = (gb_ref[...].astype(F32) * y).astype(BF16)
    for c in range(o_ref.shape[1] // chunk):
        cols = slice(c * chunk, (c + 1) * chunk)
        ya_ref[:, cols] = lax.dot_general(attn_ref[...], wap_ref[:, cols], (((0,), (0,)), ((), ())),
                                          preferred_element_type=F32)
        y_attn = ya_ref[:, cols]
        y_conv = jnp.dot(conv_ref[...], wcp_ref[:, cols], preferred_element_type=F32)
        t_attn = jnp.tanh(ga_ref[:, cols].astype(F32))
        t_conv = jnp.tanh(gg_ref[:, cols].astype(F32))
        merged_ref[:, cols] = ((y_attn + y_conv) + (t_attn * y_attn + t_conv * y_conv)).astype(BF16)
    o_ref[...] = x_ref[...] + jnp.dot(merged_ref[...], wmo_ref[...], preferred_element_type=F32)


def _mix(attn_t, proj, gate_a, x2, conv_w, wap, wcp, wmo, seq, *, tm, chunk):
    n, d = x2.shape
    cwid = wcp.shape[0]
    tps = seq // tm
    zc, gbc, gcc = COL_Z // cwid, COL_GB // cwid, COL_GC // cwid
    ggc = COL_GATE_C // d
    rb = tm // HALO
    const = lambda i: (0, 0)
    return pl.pallas_call(
        functools.partial(_mix_kernel, tm=tm, tiles_per_seq=tps, chunk=chunk),
        grid=(n // tm,),
        in_specs=[
            pl.BlockSpec((None, ATTN_WIDTH, tm), lambda i: (i // tps, 0, i % tps)),
            pl.BlockSpec((tm, cwid), lambda i: (i, zc)),
            pl.BlockSpec((tm, cwid), lambda i: (i, gbc)),
            pl.BlockSpec((tm, cwid), lambda i: (i, gcc)),
            pl.BlockSpec((HALO, cwid), lambda i: (jnp.maximum(i * rb - 1, 0), zc)),
            pl.BlockSpec((HALO, cwid), lambda i: (jnp.maximum(i * rb - 1, 0), gcc)),
            pl.BlockSpec((tm, d), lambda i: (i, 0)),
            pl.BlockSpec((tm, d), lambda i: (i, ggc)),
            pl.BlockSpec((tm, d), lambda i: (i, 0)),
            pl.BlockSpec((CONV_K, cwid), const),
            pl.BlockSpec(wap.shape, const),
            pl.BlockSpec(wcp.shape, const),
            pl.BlockSpec(wmo.shape, const),
        ],
        out_specs=pl.BlockSpec((tm, d), lambda i: (i, 0)),
        out_shape=jax.ShapeDtypeStruct((n, d), F32),
        scratch_shapes=[pltpu.VMEM((tm + HALO, cwid), F32), pltpu.VMEM((tm, cwid), BF16),
                        pltpu.VMEM((tm, d), F32), pltpu.VMEM((tm, d), BF16)],
        compiler_params=_params(("parallel",)),
        name="mix",
    )(attn_t, proj, proj, proj, proj, proj, gate_a, proj, x2, conv_w, wap, wcp, wmo)


def _memkv_kernel(m_ref, g_ref, w_ref, o_ref):
    u = _rms(m_ref[...], g_ref[...]).astype(BF16)
    o_ref[...] = jnp.dot(u, w_ref[...], preferred_element_type=F32).astype(o_ref.dtype)


def _memkv(mem2, g, w, *, tm):
    n, d = mem2.shape
    return pl.pallas_call(
        _memkv_kernel,
        grid=(n // tm,),
        in_specs=[
            pl.BlockSpec((tm, d), lambda i: (i, 0)),
            pl.BlockSpec((1, d), lambda i: (0, 0)),
            pl.BlockSpec(w.shape, lambda i: (0, 0)),
        ],
        out_specs=pl.BlockSpec((tm, w.shape[1]), lambda i: (i, 0)),
        out_shape=jax.ShapeDtypeStruct((n, w.shape[1]), BF16),
        compiler_params=_params(("parallel",)),
        name="memkv",
    )(mem2, g, w)


def _xattn_kernel(h_ref, g_ref, wq_ref, kv_ref, wo_ref, o_ref):
    h = h_ref[...]
    u = _rms(h, g_ref[...]).astype(BF16)
    q = jnp.dot(u, wq_ref[...], preferred_element_type=F32) * (X_HEAD_DIM ** -0.5)
    q = q.astype(BF16)
    xw = X_HEADS * X_HEAD_DIM
    outs = []
    for hd in range(X_HEADS):
        cols = slice(hd * X_HEAD_DIM, (hd + 1) * X_HEAD_DIM)
        kh = kv_ref[:, cols]
        vh = kv_ref[:, xw + hd * X_HEAD_DIM:xw + (hd + 1) * X_HEAD_DIM]
        s = lax.dot_general(q[:, cols], kh, (((1,), (1,)), ((), ())), preferred_element_type=F32)
        m = jnp.max(s, axis=-1, keepdims=True)
        p = jnp.exp(s - m)
        denom = jnp.sum(p, axis=-1, keepdims=True)
        outs.append((jnp.dot(p.astype(BF16), vh, preferred_element_type=F32) / denom).astype(BF16))
    o = jnp.concatenate(outs, axis=1)
    o_ref[...] = h + jnp.dot(o, wo_ref[...], preferred_element_type=F32)


def _xattn(h3, g, wq, kv3, wo, *, tm):
    b, t, d = h3.shape
    m, kvw = kv3.shape[1], kv3.shape[2]
    return pl.pallas_call(
        _xattn_kernel,
        grid=(b, t // tm),
        in_specs=[
            pl.BlockSpec((None, tm, d), lambda bi, i: (bi, i, 0)),
            pl.BlockSpec((1, d), lambda bi, i: (0, 0)),
            pl.BlockSpec(wq.shape, lambda bi, i: (0, 0)),
            pl.BlockSpec((None, m, kvw), lambda bi, i: (bi, 0, 0)),
            pl.BlockSpec(wo.shape, lambda bi, i: (0, 0)),
        ],
        out_specs=pl.BlockSpec((None, tm, d), lambda bi, i: (bi, i, 0)),
        out_shape=jax.ShapeDtypeStruct((b, t, d), F32),
        compiler_params=_params(("parallel", "parallel")),
        name="xattn",
    )(h3, g, wq, kv3, wo)


def _ffn_kernel(h_ref, g_ref, wa_ref, wb_ref, wo_ref, gf_ref, o_ref, u_ref, act_ref, *, sub, chunk):
    k = pl.program_id(1)

    last = pl.num_programs(1) - 1
    d = o_ref.shape[1]

    def hidden_tile(u, final_norm):
        th = act_ref.shape[1]
        for s in range(th // sub):
            hid = slice(s * sub, (s + 1) * sub)
            a = jnp.dot(u, wa_ref[:, hid], preferred_element_type=F32)
            b = jnp.dot(u, wb_ref[:, hid], preferred_element_type=F32)
            act_ref[:, hid] = (a * (1.0 + jnp.tanh(a)) * b).astype(BF16)
        sum_sq = jnp.zeros((o_ref.shape[0], 1), F32)
        for c in range(d // chunk):
            cols = slice(c * chunk, (c + 1) * chunk)
            y = o_ref[:, cols] + jnp.dot(act_ref[...], wo_ref[:, cols], preferred_element_type=F32)
            o_ref[:, cols] = y
            if final_norm:
                sum_sq = sum_sq + jnp.sum(y * y, axis=-1, keepdims=True)
        if final_norm:
            inv_rms = lax.rsqrt(sum_sq * (1.0 / d) + EPS)
            for c in range(d // chunk):
                cols = slice(c * chunk, (c + 1) * chunk)
                o_ref[:, cols] = o_ref[:, cols] * inv_rms * gf_ref[:, cols]

    @pl.when(k == 0)
    def _():
        h = h_ref[...]
        u = _rms(h, g_ref[...]).astype(BF16)
        u_ref[...] = u
        o_ref[...] = h
        hidden_tile(u, False)

    @pl.when((k > 0) & (k < last))
    def _():
        hidden_tile(u_ref[...], False)

    @pl.when(k == last)
    def _():
        hidden_tile(u_ref[...], True)


def _ffn(h2, g, w_in, w_out, g_final, *, tm, th, sub, chunk):
    n, d = h2.shape
    hidden = w_out.shape[0]
    nk = hidden // th
    assert nk >= 2
    return pl.pallas_call(
        functools.partial(_ffn_kernel, sub=sub, chunk=chunk),
        grid=(n // tm, nk),
        in_specs=[
            pl.BlockSpec((tm, d), lambda i, k: (i, 0)),
            pl.BlockSpec((1, d), lambda i, k: (0, 0)),
            pl.BlockSpec((d, th), lambda i, k: (0, k)),
            pl.BlockSpec((d, th), lambda i, k: (0, nk + k)),
            pl.BlockSpec((th, d), lambda i, k: (k, 0)),
            pl.BlockSpec((1, d), lambda i, k: (0, 0)),
        ],
        out_specs=pl.BlockSpec((tm, d), lambda i, k: (i, 0)),
        out_shape=jax.ShapeDtypeStruct((n, d), F32),
        scratch_shapes=[pltpu.VMEM((tm, d), BF16), pltpu.VMEM((tm, th), BF16)],
        compiler_params=_params(("parallel", "arbitrary"), FFN_VMEM_LIMIT),
        name="ffn",
    )(h2, g, w_in, w_in, w_out, g_final)


def _rope_tables(seq):
    half = HEAD_DIM // 2
    inv_freq = ROPE_THETA ** (-jnp.arange(half, dtype=F32) / half)
    ang = jnp.arange(seq, dtype=jnp.int32).astype(F32)[:, None] * inv_freq[None, :]
    cos, sin = jnp.cos(ang), jnp.sin(ang)
    reps = LANES // HEAD_DIM
    cos_t = jnp.tile(jnp.concatenate([cos, cos], axis=1), (1, reps))
    sin_t = jnp.tile(jnp.concatenate([-sin, sin], axis=1), (1, reps))
    return cos_t, sin_t


def _cast_w_in(w):
    col_scale = jnp.where(jnp.arange(w.shape[1]) >= REF_GATE_A, 0.5, 1.0).astype(F32)
    return (w * col_scale).astype(BF16)


def _col_scale(cols, value, lo=0, hi=None):
    idx = jnp.arange(cols)
    hi = cols if hi is None else hi
    return jnp.where((idx >= lo) & (idx < hi), value, 1.0).astype(F32)[None, :]


def kernel(x, mem, g_mix, w_in, conv_w, attn_sinks, w_attn_proj, w_conv_proj, w_mix_out, g_xattn, g_mem,
           w_xq, w_xkv, w_xo, g_ffn, w_ffn_in, w_ffn_out, g_final):
    b, t, d = x.shape
    depth = w_in.shape[0]
    cos_t, sin_t = _rope_tables(t)
    h = x.reshape(b * t, d)
    mem2 = mem.reshape(b * mem.shape[1], d)
    for l in range(depth):
        w_in_bf = _cast_w_in(w_in[l])
        u, kv, gate_a = _norm_proj(h, g_mix[l][None, :], w_in_bf, cos_t, sin_t, t, tm=1024, chunk=512)
        q_scale = HEAD_DIM ** -0.5 * LOG2E
        hidden = w_ffn_out.shape[1]
        side = [
            (w_ffn_in[l], _col_scale(2 * hidden, 0.5, 0, hidden), 64),
            (w_ffn_out[l], None, 128),
            (w_mix_out[l], _col_scale(d, 0.5), 64),
            (w_attn_proj[l], None, 32),
            (w_conv_proj[l], None, 32),
            (w_xq[l], None, 64),
            (w_xkv[l], None, 64),
            (w_xo[l], None, 16),
        ]
        proj, side_bf = _inproj(u, w_in_bf, cos_t * q_scale, sin_t * q_scale, t, side,
                                tm=1024, tn=2048, chunk=512)
        (w_ffn_in_bf, w_ffn_out_bf, w_mix_out_bf, w_attn_proj_bf, w_conv_proj_bf,
         w_xq_bf, w_xkv_bf, w_xo_bf) = side_bf
        attn = _swa(proj.reshape(b, t, -1), kv.reshape(b, t, -1), attn_sinks[l], tq=1024)
        h = _mix(attn, proj, gate_a, h, conv_w[l], w_attn_proj_bf, w_conv_proj_bf, w_mix_out_bf, t, tm=256, chunk=1024)
        mkv = _memkv(mem2, g_mem[l][None, :], w_xkv_bf, tm=256)
        h = _xattn(h.reshape(b, t, d), g_xattn[l][None, :], w_xq_bf,
                   mkv.reshape(b, mem.shape[1], -1), w_xo_bf, tm=1024).reshape(b * t, d)
        last = l == depth - 1
        assert last, "fused final norm expects a single layer"
        h = _ffn(h, g_ffn[l][None, :], w_ffn_in_bf, w_ffn_out_bf,
                 g_final[None, :], tm=1024, th=512, sub=256, chunk=512)
    return h.reshape(b, t, d)
```

```python
import functools

import jax
import jax.numpy as jnp
import numpy as np
from jax import lax
from jax.experimental import pallas as pl
from jax.experimental.pallas import tpu as pltpu

F32 = jnp.float32
BF16 = jnp.bfloat16

HEAD_DIM = 64
N_Q_HEADS = 16
N_KV_HEADS = 4
Q_PER_KV = N_Q_HEADS // N_KV_HEADS
ATTN_WIDTH = N_Q_HEADS * HEAD_DIM
KV_WIDTH = N_KV_HEADS * HEAD_DIM
WINDOW = 128
ROPE_THETA = 10000.0
CONV_K = 3
X_HEADS = 4
X_HEAD_DIM = 128
EPS = 1e-6
LOG2E = 1.4426950408889634

D_MODEL = 2048
CONV_WIDTH = 1024
REF_Q = 0
REF_K = REF_Q + ATTN_WIDTH
REF_V = REF_K + KV_WIDTH
REF_Z = REF_V + KV_WIDTH
REF_GB = REF_Z + CONV_WIDTH
REF_GC = REF_GB + CONV_WIDTH
REF_GATE_A = REF_GC + CONV_WIDTH
REF_GATE_C = REF_GATE_A + D_MODEL
COL_GATE_A = 0
COL_GATE_C = COL_GATE_A + D_MODEL
COL_Q = COL_GATE_C + D_MODEL
MAIN_WIDTH = COL_Q + ATTN_WIDTH
_MAIN_SRC = ((REF_GATE_A, D_MODEL), (REF_GATE_C, D_MODEL), (REF_Q, ATTN_WIDTH))

LANES = 128
HALO = 16
VMEM_LIMIT = 56 * 1024 * 1024
BIG_VMEM_LIMIT = 60 * 1024 * 1024


def _rms(x, g):
    ms = jnp.mean(x * x, axis=-1, keepdims=True)
    return x * lax.rsqrt(ms + EPS) * g


def _params(sem, vmem_limit=VMEM_LIMIT):
    return pltpu.CompilerParams(dimension_semantics=sem, vmem_limit_bytes=vmem_limit)


def _rope(acc, cos, sin_signed):
    lane = lax.broadcasted_iota(jnp.int32, (1, LANES), 1)
    first_half = (lane % HEAD_DIM) < (HEAD_DIM // 2)
    outs = []
    for c in range(acc.shape[1] // LANES):
        xc = acc[:, c * LANES:(c + 1) * LANES]
        fwd = pltpu.roll(xc, LANES - HEAD_DIM // 2, axis=1)
        bwd = pltpu.roll(xc, HEAD_DIM // 2, axis=1)
        outs.append(xc * cos + jnp.where(first_half, fwd, bwd) * sin_signed)
    return jnp.concatenate(outs, axis=1)


def _norm_proj_kernel(x_ref, g_ref, wkv_ref, *refs, tm, tiles_per_seq, chunk):
    n_chunks = CONV_WIDTH // chunk
    wz_refs, wgc_refs, wgb_refs = (refs[grp * n_chunks:(grp + 1) * n_chunks] for grp in range(3))
    cw_ref, cos_ref, sin_ref, u_ref, kv_ref, conv_ref, ext_ref = refs[3 * n_chunks:]

    @pl.when(pl.program_id(0) % tiles_per_seq == 0)
    def _():
        ext_ref[0:HALO, :] = jnp.zeros((HALO, CONV_WIDTH), F32)

    u = _rms(x_ref[...], g_ref[...]).astype(BF16)
    u_ref[...] = u
    for c in range(n_chunks):
        cols = slice(c * chunk, (c + 1) * chunk)
        z = jnp.dot(u, wz_refs[c][...], preferred_element_type=F32)
        gc = jnp.dot(u, wgc_refs[c][...], preferred_element_type=F32)
        gb = jnp.dot(u, wgb_refs[c][...], preferred_element_type=F32)
        cz = z * gc
        ext_ref[HALO:, cols] = cz
        y = (cw_ref[0:1, cols] * ext_ref[HALO - 2:HALO - 2 + tm, cols]
             + cw_ref[1:2, cols] * ext_ref[HALO - 1:HALO - 1 + tm, cols]
             + cw_ref[2:3, cols] * cz)
        conv_ref[:, cols] = (gb * y).astype(conv_ref.dtype)
        ext_ref[0:HALO, cols] = ext_ref[tm:tm + HALO, cols]
    k = jnp.dot(u, wkv_ref[:, :KV_WIDTH], preferred_element_type=F32)
    kv_ref[:, :KV_WIDTH] = _rope(k, cos_ref[...], sin_ref[...]).astype(kv_ref.dtype)
    kv_ref[:, KV_WIDTH:] = jnp.dot(u, wkv_ref[:, KV_WIDTH:], preferred_element_type=F32).astype(kv_ref.dtype)


def _norm_proj(x2, g, w_in, conv_w, cos_t, sin_t, seq, *, tm, chunk):
    n, d = x2.shape
    tpb = seq // tm
    assert REF_V == REF_K + KV_WIDTH and REF_K % (2 * KV_WIDTH) == 0
    n_chunks = CONV_WIDTH // chunk
    w_specs = []
    for ref_off in (REF_Z, REF_GC, REF_GB):
        assert ref_off % chunk == 0
        w_specs += [pl.BlockSpec((d, chunk), lambda i, blk=ref_off // chunk + c: (0, blk))
                    for c in range(n_chunks)]
    return pl.pallas_call(
        functools.partial(_norm_proj_kernel, tm=tm, tiles_per_seq=tpb, chunk=chunk),
        grid=(n // tm,),
        in_specs=[
            pl.BlockSpec((tm, d), lambda i: (i, 0)),
            pl.BlockSpec((1, d), lambda i: (0, 0)),
            pl.BlockSpec((d, 2 * KV_WIDTH), lambda i: (0, REF_K // (2 * KV_WIDTH))),
        ] + w_specs + [
            pl.BlockSpec((CONV_K, CONV_WIDTH), lambda i: (0, 0)),
            pl.BlockSpec((tm, LANES), lambda i: (i % tpb, 0)),
            pl.BlockSpec((tm, LANES), lambda i: (i % tpb, 0)),
        ],
        out_specs=[
            pl.BlockSpec((tm, d), lambda i: (i, 0)),
            pl.BlockSpec((tm, 2 * KV_WIDTH), lambda i: (i, 0)),
            pl.BlockSpec((tm, CONV_WIDTH), lambda i: (i, 0)),
        ],
        out_shape=[
            jax.ShapeDtypeStruct((n, d), BF16),
            jax.ShapeDtypeStruct((n, 2 * KV_WIDTH), BF16),
            jax.ShapeDtypeStruct((n, CONV_WIDTH), BF16),
        ],
        scratch_shapes=[pltpu.VMEM((tm + HALO, CONV_WIDTH), F32)],
        compiler_params=_params(("arbitrary",)),
        name="norm_proj",
    )(x2, g, w_in, *([w_in] * (3 * n_chunks)), conv_w, cos_t, sin_t)


def _inproj_kernel(u_ref, *refs, tn, chunk, side_scaled):
    n_chunks = tn // chunk
    w_refs, refs = refs[:n_chunks], refs[n_chunks:]
    (cos_ref, sin_ref), refs = refs[:2], refs[2:]
    n_side_in = len(side_scaled) + sum(side_scaled)
    side_in, (o_ref, *side_out) = list(refs[:n_side_in]), refs[n_side_in:]
    j = pl.program_id(0)
    for out_ref, scaled in zip(side_out, side_scaled):
        w = side_in.pop(0)[...]
        if scaled:
            w = w * side_in.pop(0)[...]
        out_ref[...] = w.astype(out_ref.dtype)
    n_tiles = MAIN_WIDTH // tn
    q_tile, q_lo = COL_Q // tn, COL_Q % tn
    assert q_lo % chunk == 0 and MAIN_WIDTH == (q_tile + 1) * tn

    def tile(rope_from):
        for c in range(n_chunks):
            acc = jnp.dot(u_ref[...], w_refs[c][...], preferred_element_type=F32)
            if c * chunk >= rope_from:
                acc = _rope(acc, cos_ref[...], sin_ref[...])
            o_ref[:, c * chunk:(c + 1) * chunk] = acc.astype(o_ref.dtype)

    pl.when(j < n_tiles - 1)(lambda: tile(tn))
    pl.when(j == n_tiles - 1)(lambda: tile(q_lo))


def _inproj(u, w_in, cos_t, sin_t, seq, side, *, tm, tn, chunk):
    n, d = u.shape
    tpb = seq // tm
    n_tiles, n_chunks = MAIN_WIDTH // tn, tn // chunk
    n_rows = n // tm

    side_in_specs, side_out_specs, side_out_shapes, side_args = [], [], [], []
    for w, scale, rows in side:
        n_blocks = w.shape[0] // rows
        assert w.shape[0] % rows == 0 and rows % HALO == 0 and n_blocks <= n_tiles * n_rows
        index = lambda j, i, n_blocks=n_blocks: (jnp.minimum(j * n_rows + i, n_blocks - 1), 0)
        side_in_specs.append(pl.BlockSpec((rows, w.shape[1]), index))
        side_args.append(w)
        if scale is not None:
            side_in_specs.append(pl.BlockSpec((1, w.shape[1]), lambda j, i: (0, 0)))
            side_args.append(scale)
        side_out_specs.append(pl.BlockSpec((rows, w.shape[1]), index))
        side_out_shapes.append(jax.ShapeDtypeStruct(w.shape, BF16))
    src_blocks = []
    for ref_off, width in _MAIN_SRC:
        assert ref_off % chunk == 0 and width % chunk == 0
        src_blocks += [ref_off // chunk + b for b in range(width // chunk)]
    assert len(src_blocks) == n_tiles * n_chunks

    def w_spec(c):
        def index(j, i):
            blk = 0
            for jj in range(n_tiles):
                blk = blk + jnp.where(j == jj, src_blocks[jj * n_chunks + c], 0)
            return (0, blk)
        return pl.BlockSpec((d, chunk), index)

    proj, *side_bf = pl.pallas_call(
        functools.partial(_inproj_kernel, tn=tn, chunk=chunk,
                          side_scaled=tuple(scale is not None for _, scale, _ in side)),
        grid=(n_tiles, n_rows),
        in_specs=[pl.BlockSpec((tm, d), lambda j, i: (i, 0))]
        + [w_spec(c) for c in range(n_chunks)]
        + [pl.BlockSpec((tm, LANES), lambda j, i: (i % tpb, 0)),
           pl.BlockSpec((tm, LANES), lambda j, i: (i % tpb, 0))]
        + side_in_specs,
        out_specs=[pl.BlockSpec((tm, tn), lambda j, i: (i, j))] + side_out_specs,
        out_shape=[jax.ShapeDtypeStruct((n, MAIN_WIDTH), BF16)] + side_out_shapes,
        compiler_params=_params(("arbitrary", "arbitrary"), BIG_VMEM_LIMIT),
        name="inproj",
    )(u, *([w_in] * n_chunks), cos_t, sin_t, *side_args)
    return proj, side_bf


def _swa_kernel(sink_ref, q_ref, k_ref, v_ref, kp_ref, vp_ref, o_ref, bias_ref, st_ref, *, nblk):
    i = pl.program_id(1)
    stacked = Q_PER_KV * WINDOW
    key = lax.broadcasted_iota(jnp.int32, (2 * WINDOW, stacked), 0)
    qry = lax.broadcasted_iota(jnp.int32, (2 * WINDOW, stacked), 1) % WINDOW
    band = (key > qry) & (key <= qry + WINDOW)
    bias_ref[0] = jnp.where(band & ((key >= WINDOW) | (i > 0)), 0.0, -jnp.inf)
    bias_ref[1] = jnp.where(band, 0.0, -jnp.inf)
    lane_head = lax.broadcasted_iota(jnp.int32, (1, stacked), 1) // WINDOW

    def band_of(ref, prev_ref, b, g):
        cols = slice(g * HEAD_DIM, (g + 1) * HEAD_DIM)
        if b == 0:
            return jnp.concatenate([prev_ref[:, cols], ref[0:WINDOW, cols]], axis=0)
        return ref[(b - 1) * WINDOW:(b + 1) * WINDOW, cols]

    def scores(b, g):
        rows = slice(b * WINDOW, (b + 1) * WINDOW)
        heads = range(g * Q_PER_KV, (g + 1) * Q_PER_KV)
        qs = jnp.concatenate([q_ref[rows, h * HEAD_DIM:(h + 1) * HEAD_DIM] for h in heads], axis=0)
        kh = band_of(k_ref, kp_ref, b, g)
        return lax.dot_general(kh, qs, (((1,), (1,)), ((), ())), preferred_element_type=F32)

    def finish(b, g, st):
        rows = slice(b * WINDOW, (b + 1) * WINDOW)
        heads = range(g * Q_PER_KV, (g + 1) * Q_PER_KV)
        sink = jnp.zeros((1, stacked), F32)
        for n, h in enumerate(heads):
            sink = jnp.where(lane_head == n, sink_ref[h] * LOG2E, sink)
        st = st + bias_ref[0 if b == 0 else 1]
        m = jnp.maximum(jnp.max(st, axis=0, keepdims=True), sink)
        p = jnp.exp2(st - m)
        denom = jnp.sum(p, axis=0, keepdims=True) + jnp.exp2(sink - m)
        vh = band_of(v_ref, vp_ref, b, g)
        ot = lax.dot_general(vh, p.astype(BF16), (((0,), (0,)), ((), ())), preferred_element_type=F32)
        ot = ot * (1.0 / denom)
        for n, h in enumerate(heads):
            o_ref[h * HEAD_DIM:(h + 1) * HEAD_DIM, rows] = ot[:, n * WINDOW:(n + 1) * WINDOW].astype(o_ref.dtype)

    pairs = [(b, g) for b in range(nblk) for g in range(N_KV_HEADS)]
    ahead, slots = 2, 3
    for n in range(ahead):
        st_ref[n] = scores(*pairs[n])
    for n, (b, g) in enumerate(pairs):
        if n + ahead < len(pairs):
            st_ref[(n + ahead) % slots] = scores(*pairs[n + ahead])
        finish(b, g, st_ref[n % slots])


def _swa(proj3, kv3, sinks, *, tq):
    b, t, _ = proj3.shape
    nblk = tq // WINDOW
    qcol = COL_Q // ATTN_WIDTH
    prev = lambda bi, i: jnp.maximum(i * nblk - 1, 0)
    return pl.pallas_call(
        functools.partial(_swa_kernel, nblk=nblk),
        grid=(b, t // tq),
        in_specs=[
            pl.BlockSpec(memory_space=pltpu.SMEM),
            pl.BlockSpec((None, tq, ATTN_WIDTH), lambda bi, i: (bi, i, qcol)),
            pl.BlockSpec((None, tq, KV_WIDTH), lambda bi, i: (bi, i, 0)),
            pl.BlockSpec((None, tq, KV_WIDTH), lambda bi, i: (bi, i, 1)),
            pl.BlockSpec((None, WINDOW, KV_WIDTH), lambda bi, i: (bi, prev(bi, i), 0)),
            pl.BlockSpec((None, WINDOW, KV_WIDTH), lambda bi, i: (bi, prev(bi, i), 1)),
        ],
        out_specs=pl.BlockSpec((None, ATTN_WIDTH, tq), lambda bi, i: (bi, 0, i)),
        out_shape=jax.ShapeDtypeStruct((b, ATTN_WIDTH, t), BF16),
        scratch_shapes=[pltpu.VMEM((2, 2 * WINDOW, Q_PER_KV * WINDOW), F32),
                        pltpu.VMEM((3, 2 * WINDOW, Q_PER_KV * WINDOW), F32)],
        compiler_params=_params(("parallel", "arbitrary")),
        name="swa",
    )(sinks, proj3, kv3, kv3, kv3, kv3)


def _mix_kernel(attn_ref, conv_ref, ga_ref, gg_ref, x_ref, wap_ref, wcp_ref, wmo_ref, o_ref,
                ya_ref, merged_ref, *, chunk):
    for c in range(o_ref.shape[1] // chunk):
        cols = slice(c * chunk, (c + 1) * chunk)
        ya_ref[:, cols] = lax.dot_general(attn_ref[...], wap_ref[:, cols], (((0,), (0,)), ((), ())),
                                          preferred_element_type=F32)
        y_attn = ya_ref[:, cols]
        y_conv = jnp.dot(conv_ref[...], wcp_ref[:, cols], preferred_element_type=F32)
        t_attn = jnp.tanh(ga_ref[:, cols].astype(F32))
        t_conv = jnp.tanh(gg_ref[:, cols].astype(F32))
        merged_ref[:, cols] = ((y_attn + y_conv) + (t_attn * y_attn + t_conv * y_conv)).astype(BF16)
    o_ref[...] = x_ref[...] + jnp.dot(merged_ref[...], wmo_ref[...], preferred_element_type=F32)


def _mix(attn_t, conv, proj, x2, wap, wcp, wmo, seq, *, tm, chunk):
    n, d = x2.shape
    tps = seq // tm
    gac, ggc = COL_GATE_A // d, COL_GATE_C // d
    const = lambda i: (0, 0)
    return pl.pallas_call(
        functools.partial(_mix_kernel, chunk=chunk),
        grid=(n // tm,),
        in_specs=[
            pl.BlockSpec((None, ATTN_WIDTH, tm), lambda i: (i // tps, 0, i % tps)),
            pl.BlockSpec((tm, conv.shape[1]), lambda i: (i, 0)),
            pl.BlockSpec((tm, d), lambda i: (i, gac)),
            pl.BlockSpec((tm, d), lambda i: (i, ggc)),
            pl.BlockSpec((tm, d), lambda i: (i, 0)),
            pl.BlockSpec(wap.shape, const),
            pl.BlockSpec(wcp.shape, const),
            pl.BlockSpec(wmo.shape, const),
        ],
        out_specs=pl.BlockSpec((tm, d), lambda i: (i, 0)),
        out_shape=jax.ShapeDtypeStruct((n, d), F32),
        scratch_shapes=[pltpu.VMEM((tm, d), F32), pltpu.VMEM((tm, d), BF16)],
        compiler_params=_params(("parallel",)),
        name="mix",
    )(attn_t, conv, proj, proj, x2, wap, wcp, wmo)


def _memkv_kernel(m_ref, g_ref, w_ref, o_ref):
    u = _rms(m_ref[...], g_ref[...]).astype(BF16)
    o_ref[...] = jnp.dot(u, w_ref[...], preferred_element_type=F32).astype(o_ref.dtype)


def _memkv(mem2, g, w, *, tm):
    n, d = mem2.shape
    return pl.pallas_call(
        _memkv_kernel,
        grid=(n // tm,),
        in_specs=[
            pl.BlockSpec((tm, d), lambda i: (i, 0)),
            pl.BlockSpec((1, d), lambda i: (0, 0)),
            pl.BlockSpec(w.shape, lambda i: (0, 0)),
        ],
        out_specs=pl.BlockSpec((tm, w.shape[1]), lambda i: (i, 0)),
        out_shape=jax.ShapeDtypeStruct((n, w.shape[1]), BF16),
        compiler_params=_params(("parallel",)),
        name="memkv",
    )(mem2, g, w)


def _xattn_kernel(h_ref, g_ref, wq_ref, kv_ref, wo_ref, o_ref):
    h = h_ref[...]
    u = _rms(h, g_ref[...]).astype(BF16)
    q = jnp.dot(u, wq_ref[...], preferred_element_type=F32) * (X_HEAD_DIM ** -0.5)
    q = q.astype(BF16)
    xw = X_HEADS * X_HEAD_DIM
    outs = []
    for hd in range(X_HEADS):
        cols = slice(hd * X_HEAD_DIM, (hd + 1) * X_HEAD_DIM)
        kh = kv_ref[:, cols]
        vh = kv_ref[:, xw + hd * X_HEAD_DIM:xw + (hd + 1) * X_HEAD_DIM]
        s = lax.dot_general(q[:, cols], kh, (((1,), (1,)), ((), ())), preferred_element_type=F32)
        m = jnp.max(s, axis=-1, keepdims=True)
        p = jnp.exp(s - m)
        denom = jnp.sum(p, axis=-1, keepdims=True)
        outs.append((jnp.dot(p.astype(BF16), vh, preferred_element_type=F32) / denom).astype(BF16))
    o = jnp.concatenate(outs, axis=1)
    o_ref[...] = h + jnp.dot(o, wo_ref[...], preferred_element_type=F32)


def _xattn(h3, g, wq, kv3, wo, *, tm):
    b, t, d = h3.shape
    m, kvw = kv3.shape[1], kv3.shape[2]
    return pl.pallas_call(
        _xattn_kernel,
        grid=(b, t // tm),
        in_specs=[
            pl.BlockSpec((None, tm, d), lambda bi, i: (bi, i, 0)),
            pl.BlockSpec((1, d), lambda bi, i: (0, 0)),
            pl.BlockSpec(wq.shape, lambda bi, i: (0, 0)),
            pl.BlockSpec((None, m, kvw), lambda bi, i: (bi, 0, 0)),
            pl.BlockSpec(wo.shape, lambda bi, i: (0, 0)),
        ],
        out_specs=pl.BlockSpec((None, tm, d), lambda bi, i: (bi, i, 0)),
        out_shape=jax.ShapeDtypeStruct((b, t, d), F32),
        compiler_params=_params(("parallel", "parallel")),
        name="xattn",
    )(h3, g, wq, kv3, wo)


def _ffn_kernel(h_ref, g_ref, wa_ref, wb_ref, wo_ref, gf_ref, o_ref, u_ref, act_ref, *, sub, chunk):
    k = pl.program_id(1)

    last = pl.num_programs(1) - 1
    d = o_ref.shape[1]

    def hidden_tile(u, final_norm):
        th = act_ref.shape[1]
        for s in range(th // sub):
            hid = slice(s * sub, (s + 1) * sub)
            a = jnp.dot(u, wa_ref[:, hid], preferred_element_type=F32)
            b = jnp.dot(u, wb_ref[:, hid], preferred_element_type=F32)
            act_ref[:, hid] = (a * (1.0 + jnp.tanh(a)) * b).astype(BF16)
        sum_sq = jnp.zeros((o_ref.shape[0], 1), F32)
        for c in range(d // chunk):
            cols = slice(c * chunk, (c + 1) * chunk)
            y = o_ref[:, cols] + jnp.dot(act_ref[...], wo_ref[:, cols], preferred_element_type=F32)
            o_ref[:, cols] = y
            if final_norm:
                sum_sq = sum_sq + jnp.sum(y * y, axis=-1, keepdims=True)
        if final_norm:
            inv_rms = lax.rsqrt(sum_sq * (1.0 / d) + EPS)
            for c in range(d // chunk):
                cols = slice(c * chunk, (c + 1) * chunk)
                o_ref[:, cols] = o_ref[:, cols] * inv_rms * gf_ref[:, cols]

    @pl.when(k == 0)
    def _():
        h = h_ref[...]
        u = _rms(h, g_ref[...]).astype(BF16)
        u_ref[...] = u
        o_ref[...] = h
        hidden_tile(u, False)

    @pl.when((k > 0) & (k < last))
    def _():
        hidden_tile(u_ref[...], False)

    @pl.when(k == last)
    def _():
        hidden_tile(u_ref[...], True)


def _ffn(h2, g, w_in, w_out, g_final, *, tm, th, sub, chunk):
    n, d = h2.shape
    hidden = w_out.shape[0]
    nk = hidden // th
    assert nk >= 2
    return pl.pallas_call(
        functools.partial(_ffn_kernel, sub=sub, chunk=chunk),
        grid=(n // tm, nk),
        in_specs=[
            pl.BlockSpec((tm, d), lambda i, k: (i, 0)),
            pl.BlockSpec((1, d), lambda i, k: (0, 0)),
            pl.BlockSpec((d, th), lambda i, k: (0, k)),
            pl.BlockSpec((d, th), lambda i, k: (0, nk + k)),
            pl.BlockSpec((th, d), lambda i, k: (k, 0)),
            pl.BlockSpec((1, d), lambda i, k: (0, 0)),
        ],
        out_specs=pl.BlockSpec((tm, d), lambda i, k: (i, 0)),
        out_shape=jax.ShapeDtypeStruct((n, d), F32),
        scratch_shapes=[pltpu.VMEM((tm, d), BF16), pltpu.VMEM((tm, th), BF16)],
        compiler_params=_params(("parallel", "arbitrary"), BIG_VMEM_LIMIT),
        name="ffn",
    )(h2, g, w_in, w_in, w_out, g_final)


def _rope_tables(seq):
    half = HEAD_DIM // 2
    inv_freq = ROPE_THETA ** (-jnp.arange(half, dtype=F32) / half)
    ang = jnp.arange(seq, dtype=jnp.int32).astype(F32)[:, None] * inv_freq[None, :]
    cos, sin = jnp.cos(ang), jnp.sin(ang)
    reps = LANES // HEAD_DIM
    cos_t = jnp.tile(jnp.concatenate([cos, cos], axis=1), (1, reps))
    sin_t = jnp.tile(jnp.concatenate([-sin, sin], axis=1), (1, reps))
    return cos_t, sin_t


def _cast_w_in(w):
    col_scale = jnp.where(jnp.arange(w.shape[1]) >= REF_GATE_A, 0.5, 1.0).astype(F32)
    return (w * col_scale).astype(BF16)


def _col_scale(cols, value, lo=0, hi=None):
    idx = jnp.arange(cols)
    hi = cols if hi is None else hi
    return jnp.where((idx >= lo) & (idx < hi), value, 1.0).astype(F32)[None, :]


def kernel(x, mem, g_mix, w_in, conv_w, attn_sinks, w_attn_proj, w_conv_proj, w_mix_out, g_xattn, g_mem,
           w_xq, w_xkv, w_xo, g_ffn, w_ffn_in, w_ffn_out, g_final):
    b, t, d = x.shape
    depth = w_in.shape[0]
    cos_t, sin_t = _rope_tables(t)
    h = x.reshape(b * t, d)
    mem2 = mem.reshape(b * mem.shape[1], d)
    for l in range(depth):
        w_in_bf = _cast_w_in(w_in[l])
        u, kv, conv = _norm_proj(h, g_mix[l][None, :], w_in_bf, conv_w[l], cos_t, sin_t, t, tm=512, chunk=512)
        q_scale = HEAD_DIM ** -0.5 * LOG2E
        hidden = w_ffn_out.shape[1]
        side = [
            (w_ffn_in[l], _col_scale(2 * hidden, 0.5, 0, hidden), 64),
            (w_ffn_out[l], None, 176),
            (w_mix_out[l], _col_scale(d, 0.5), 64),
            (w_attn_proj[l], None, 32),
            (w_conv_proj[l], None, 32),
            (w_xq[l], None, 64),
            (w_xkv[l], None, 64),
            (w_xo[l], None, 16),
        ]
        proj, side_bf = _inproj(u, w_in_bf, cos_t * q_scale, sin_t * q_scale, t, side,
                                tm=1024, tn=MAIN_WIDTH // 2, chunk=512)
        (w_ffn_in_bf, w_ffn_out_bf, w_mix_out_bf, w_attn_proj_bf, w_conv_proj_bf,
         w_xq_bf, w_xkv_bf, w_xo_bf) = side_bf
        attn = _swa(proj.reshape(b, t, -1), kv.reshape(b, t, -1), attn_sinks[l], tq=1024)
        h = _mix(attn, conv, proj, h, w_attn_proj_bf, w_conv_proj_bf, w_mix_out_bf, t, tm=256, chunk=1024)
        mkv = _memkv(mem2, g_mem[l][None, :], w_xkv_bf, tm=256)
        h = _xattn(h.reshape(b, t, d), g_xattn[l][None, :], w_xq_bf,
                   mkv.reshape(b, mem.shape[1], -1), w_xo_bf, tm=1024).reshape(b * t, d)
        last = l == depth - 1
        assert last, "fused final norm expects a single layer"
        h = _ffn(h, g_ffn[l][None, :], w_ffn_in_bf, w_ffn_out_bf,
                 g_final[None, :], tm=1024, th=512, sub=256, chunk=512)
    return h.reshape(b, t, d)
```

```python
import functools

import jax
import jax.numpy as jnp
import numpy as np
from jax import lax
from jax.experimental import pallas as pl
from jax.experimental.pallas import tpu as pltpu

F32 = jnp.float32
BF16 = jnp.bfloat16

HEAD_DIM = 64
N_Q_HEADS = 16
N_KV_HEADS = 4
Q_PER_KV = N_Q_HEADS // N_KV_HEADS
ATTN_WIDTH = N_Q_HEADS * HEAD_DIM
KV_WIDTH = N_KV_HEADS * HEAD_DIM
WINDOW = 128
ROPE_THETA = 10000.0
CONV_K = 3
X_HEADS = 4
X_HEAD_DIM = 128
EPS = 1e-6
LOG2E = 1.4426950408889634

D_MODEL = 2048
CONV_WIDTH = 1024
REF_Q = 0
REF_K = REF_Q + ATTN_WIDTH
REF_V = REF_K + KV_WIDTH
REF_Z = REF_V + KV_WIDTH
REF_GB = REF_Z + CONV_WIDTH
REF_GC = REF_GB + CONV_WIDTH
REF_GATE_A = REF_GC + CONV_WIDTH
REF_GATE_C = REF_GATE_A + D_MODEL
COL_GATE_A = 0
COL_GATE_C = COL_GATE_A + D_MODEL
COL_Q = COL_GATE_C + D_MODEL
MAIN_WIDTH = COL_Q + ATTN_WIDTH
_MAIN_SRC = ((REF_GATE_A, D_MODEL), (REF_GATE_C, D_MODEL), (REF_Q, ATTN_WIDTH))

LANES = 128
HALO = 16
VMEM_LIMIT = 56 * 1024 * 1024
BIG_VMEM_LIMIT = 60 * 1024 * 1024


def _rms(x, g):
    ms = jnp.mean(x * x, axis=-1, keepdims=True)
    return x * lax.rsqrt(ms + EPS) * g


def _params(sem, vmem_limit=VMEM_LIMIT):
    return pltpu.CompilerParams(dimension_semantics=sem, vmem_limit_bytes=vmem_limit)


def _rope(acc, cos, sin_signed):
    lane = lax.broadcasted_iota(jnp.int32, (1, LANES), 1)
    first_half = (lane % HEAD_DIM) < (HEAD_DIM // 2)
    outs = []
    for c in range(acc.shape[1] // LANES):
        xc = acc[:, c * LANES:(c + 1) * LANES]
        fwd = pltpu.roll(xc, LANES - HEAD_DIM // 2, axis=1)
        bwd = pltpu.roll(xc, HEAD_DIM // 2, axis=1)
        outs.append(xc * cos + jnp.where(first_half, fwd, bwd) * sin_signed)
    return jnp.concatenate(outs, axis=1)


def _norm_proj_kernel(x_ref, g_ref, wkv_ref, *refs, tm, tiles_per_seq, chunk):
    n_chunks = CONV_WIDTH // chunk
    wz_refs, wgc_refs, wgb_refs = (refs[grp * n_chunks:(grp + 1) * n_chunks] for grp in range(3))
    cw_ref, cos_ref, sin_ref, u_ref, kv_ref, conv_ref, ext_ref = refs[3 * n_chunks:]

    @pl.when(pl.program_id(0) % tiles_per_seq == 0)
    def _():
        ext_ref[0:HALO, :] = jnp.zeros((HALO, CONV_WIDTH), F32)

    u = _rms(x_ref[...], g_ref[...]).astype(BF16)
    u_ref[...] = u
    for c in range(n_chunks):
        cols = slice(c * chunk, (c + 1) * chunk)
        z = jnp.dot(u, wz_refs[c][...], preferred_element_type=F32)
        gc = jnp.dot(u, wgc_refs[c][...], preferred_element_type=F32)
        gb = jnp.dot(u, wgb_refs[c][...], preferred_element_type=F32)
        cz = z * gc
        ext_ref[HALO:, cols] = cz
        y = (cw_ref[0:1, cols] * ext_ref[HALO - 2:HALO - 2 + tm, cols]
             + cw_ref[1:2, cols] * ext_ref[HALO - 1:HALO - 1 + tm, cols]
             + cw_ref[2:3, cols] * cz)
        conv_ref[:, cols] = (gb * y).astype(conv_ref.dtype)
        ext_ref[0:HALO, cols] = ext_ref[tm:tm + HALO, cols]
    k = jnp.dot(u, wkv_ref[:, :KV_WIDTH], preferred_element_type=F32)
    kv_ref[:, :KV_WIDTH] = _rope(k, cos_ref[...], sin_ref[...]).astype(kv_ref.dtype)
    kv_ref[:, KV_WIDTH:] = jnp.dot(u, wkv_ref[:, KV_WIDTH:], preferred_element_type=F32).astype(kv_ref.dtype)


def _norm_proj(x2, g, w_in, conv_w, cos_t, sin_t, seq, *, tm, chunk):
    n, d = x2.shape
    tpb = seq // tm
    assert REF_V == REF_K + KV_WIDTH and REF_K % (2 * KV_WIDTH) == 0
    n_chunks = CONV_WIDTH // chunk
    w_specs = []
    for ref_off in (REF_Z, REF_GC, REF_GB):
        assert ref_off % chunk == 0
        w_specs += [pl.BlockSpec((d, chunk), lambda i, blk=ref_off // chunk + c: (0, blk),
                                 pipeline_mode=pl.Buffered(1))
                    for c in range(n_chunks)]
    return pl.pallas_call(
        functools.partial(_norm_proj_kernel, tm=tm, tiles_per_seq=tpb, chunk=chunk),
        grid=(n // tm,),
        in_specs=[
            pl.BlockSpec((tm, d), lambda i: (i, 0)),
            pl.BlockSpec((1, d), lambda i: (0, 0)),
            pl.BlockSpec((d, 2 * KV_WIDTH), lambda i: (0, REF_K // (2 * KV_WIDTH)),
                         pipeline_mode=pl.Buffered(1)),
        ] + w_specs + [
            pl.BlockSpec((CONV_K, CONV_WIDTH), lambda i: (0, 0)),
            pl.BlockSpec((tm, LANES), lambda i: (i % tpb, 0)),
            pl.BlockSpec((tm, LANES), lambda i: (i % tpb, 0)),
        ],
        out_specs=[
            pl.BlockSpec((tm, d), lambda i: (i, 0)),
            pl.BlockSpec((tm, 2 * KV_WIDTH), lambda i: (i, 0)),
            pl.BlockSpec((tm, CONV_WIDTH), lambda i: (i, 0)),
        ],
        out_shape=[
            jax.ShapeDtypeStruct((n, d), BF16),
            jax.ShapeDtypeStruct((n, 2 * KV_WIDTH), BF16),
            jax.ShapeDtypeStruct((n, CONV_WIDTH), BF16),
        ],
        scratch_shapes=[pltpu.VMEM((tm + HALO, CONV_WIDTH), F32)],
        compiler_params=_params(("arbitrary",), BIG_VMEM_LIMIT),
        name="norm_proj",
    )(x2, g, w_in, *([w_in] * (3 * n_chunks)), conv_w, cos_t, sin_t)


def _inproj_kernel(u_ref, *refs, tn, chunk, side_scaled):
    n_chunks = tn // chunk
    w_refs, refs = refs[:n_chunks], refs[n_chunks:]
    (cos_ref, sin_ref), refs = refs[:2], refs[2:]
    n_side_in = len(side_scaled) + sum(side_scaled)
    side_in, (o_ref, *side_out) = list(refs[:n_side_in]), refs[n_side_in:]
    j = pl.program_id(0)
    for out_ref, scaled in zip(side_out, side_scaled):
        w = side_in.pop(0)[...]
        if scaled:
            w = w * side_in.pop(0)[...]
        out_ref[...] = w.astype(out_ref.dtype)
    n_tiles = MAIN_WIDTH // tn
    q_tile, q_lo = COL_Q // tn, COL_Q % tn
    assert q_lo % chunk == 0 and MAIN_WIDTH == (q_tile + 1) * tn

    def tile(rope_from):
        for c in range(n_chunks):
            acc = jnp.dot(u_ref[...], w_refs[c][...], preferred_element_type=F32)
            if c * chunk >= rope_from:
                acc = _rope(acc, cos_ref[...], sin_ref[...])
            o_ref[:, c * chunk:(c + 1) * chunk] = acc.astype(o_ref.dtype)

    pl.when(j < n_tiles - 1)(lambda: tile(tn))
    pl.when(j == n_tiles - 1)(lambda: tile(q_lo))


def _inproj(u, w_in, cos_t, sin_t, seq, side, *, tm, tn, chunk):
    n, d = u.shape
    tpb = seq // tm
    n_tiles, n_chunks = MAIN_WIDTH // tn, tn // chunk
    n_rows = n // tm

    side_in_specs, side_out_specs, side_out_shapes, side_args = [], [], [], []
    for w, scale, rows in side:
        n_blocks = w.shape[0] // rows
        assert w.shape[0] % rows == 0 and rows % HALO == 0 and n_blocks <= n_tiles * n_rows
        index = lambda j, i, n_blocks=n_blocks: (jnp.minimum(j * n_rows + i, n_blocks - 1), 0)
        side_in_specs.append(pl.BlockSpec((rows, w.shape[1]), index))
        side_args.append(w)
        if scale is not None:
            side_in_specs.append(pl.BlockSpec((1, w.shape[1]), lambda j, i: (0, 0)))
            side_args.append(scale)
        side_out_specs.append(pl.BlockSpec((rows, w.shape[1]), index))
        side_out_shapes.append(jax.ShapeDtypeStruct(w.shape, BF16))
    src_blocks = []
    for ref_off, width in _MAIN_SRC:
        assert ref_off % chunk == 0 and width % chunk == 0
        src_blocks += [ref_off // chunk + b for b in range(width // chunk)]
    assert len(src_blocks) == n_tiles * n_chunks

    def w_spec(c):
        def index(j, i):
            blk = 0
            for jj in range(n_tiles):
                blk = blk + jnp.where(j == jj, src_blocks[jj * n_chunks + c], 0)
            return (0, blk)
        return pl.BlockSpec((d, chunk), index)

    proj, *side_bf = pl.pallas_call(
        functools.partial(_inproj_kernel, tn=tn, chunk=chunk,
                          side_scaled=tuple(scale is not None for _, scale, _ in side)),
        grid=(n_tiles, n_rows),
        in_specs=[pl.BlockSpec((tm, d), lambda j, i: (i, 0))]
        + [w_spec(c) for c in range(n_chunks)]
        + [pl.BlockSpec((tm, LANES), lambda j, i: (i % tpb, 0)),
           pl.BlockSpec((tm, LANES), lambda j, i: (i % tpb, 0))]
        + side_in_specs,
        out_specs=[pl.BlockSpec((tm, tn), lambda j, i: (i, j))] + side_out_specs,
        out_shape=[jax.ShapeDtypeStruct((n, MAIN_WIDTH), BF16)] + side_out_shapes,
        compiler_params=_params(("arbitrary", "arbitrary"), BIG_VMEM_LIMIT),
        name="inproj",
    )(u, *([w_in] * n_chunks), cos_t, sin_t, *side_args)
    return proj, side_bf


def _swa_kernel(sink_ref, q_ref, k_ref, v_ref, kp_ref, vp_ref, o_ref, bias_ref, st_ref, *, nblk):
    i = pl.program_id(1)
    stacked = Q_PER_KV * WINDOW
    key = lax.broadcasted_iota(jnp.int32, (2 * WINDOW, stacked), 0)
    qry = lax.broadcasted_iota(jnp.int32, (2 * WINDOW, stacked), 1) % WINDOW
    band = (key > qry) & (key <= qry + WINDOW)
    bias_ref[0] = jnp.where(band & ((key >= WINDOW) | (i > 0)), 0.0, -jnp.inf)
    bias_ref[1] = jnp.where(band, 0.0, -jnp.inf)
    lane_head = lax.broadcasted_iota(jnp.int32, (1, stacked), 1) // WINDOW

    def band_of(ref, prev_ref, b, g):
        cols = slice(g * HEAD_DIM, (g + 1) * HEAD_DIM)
        if b == 0:
            return jnp.concatenate([prev_ref[:, cols], ref[0:WINDOW, cols]], axis=0)
        return ref[(b - 1) * WINDOW:(b + 1) * WINDOW, cols]

    def scores(b, g):
        rows = slice(b * WINDOW, (b + 1) * WINDOW)
        heads = range(g * Q_PER_KV, (g + 1) * Q_PER_KV)
        qs = jnp.concatenate([q_ref[rows, h * HEAD_DIM:(h + 1) * HEAD_DIM] for h in heads], axis=0)
        kh = band_of(k_ref, kp_ref, b, g)
        return lax.dot_general(kh, qs, (((1,), (1,)), ((), ())), preferred_element_type=F32)

    def finish(b, g, st):
        rows = slice(b * WINDOW, (b + 1) * WINDOW)
        heads = range(g * Q_PER_KV, (g + 1) * Q_PER_KV)
        sink = jnp.zeros((1, stacked), F32)
        for n, h in enumerate(heads):
            sink = jnp.where(lane_head == n, sink_ref[h] * LOG2E, sink)
        st = st + bias_ref[0 if b == 0 else 1]
        m = jnp.maximum(jnp.max(st, axis=0, keepdims=True), sink)
        p = jnp.exp2(st - m)
        denom = jnp.sum(p, axis=0, keepdims=True) + jnp.exp2(sink - m)
        vh = band_of(v_ref, vp_ref, b, g)
        ot = lax.dot_general(vh, p.astype(BF16), (((0,), (0,)), ((), ())), preferred_element_type=F32)
        ot = ot * (1.0 / denom)
        for n, h in enumerate(heads):
            o_ref[h * HEAD_DIM:(h + 1) * HEAD_DIM, rows] = ot[:, n * WINDOW:(n + 1) * WINDOW].astype(o_ref.dtype)

    pairs = [(b, g) for b in range(nblk) for g in range(N_KV_HEADS)]
    ahead, slots = 2, st_ref.shape[0]
    assert slots > ahead
    for n in range(ahead):
        st_ref[n] = scores(*pairs[n])
    for n, (b, g) in enumerate(pairs):
        if n + ahead < len(pairs):
            st_ref[(n + ahead) % slots] = scores(*pairs[n + ahead])
        finish(b, g, st_ref[n % slots])


def _swa(proj3, kv3, sinks, *, tq):
    b, t, _ = proj3.shape
    nblk = tq // WINDOW
    qcol = COL_Q // ATTN_WIDTH
    prev = lambda bi, i: jnp.maximum(i * nblk - 1, 0)
    return pl.pallas_call(
        functools.partial(_swa_kernel, nblk=nblk),
        grid=(b, t // tq),
        in_specs=[
            pl.BlockSpec(memory_space=pltpu.SMEM),
            pl.BlockSpec((None, tq, ATTN_WIDTH), lambda bi, i: (bi, i, qcol)),
            pl.BlockSpec((None, tq, KV_WIDTH), lambda bi, i: (bi, i, 0)),
            pl.BlockSpec((None, tq, KV_WIDTH), lambda bi, i: (bi, i, 1)),
            pl.BlockSpec((None, WINDOW, KV_WIDTH), lambda bi, i: (bi, prev(bi, i), 0)),
            pl.BlockSpec((None, WINDOW, KV_WIDTH), lambda bi, i: (bi, prev(bi, i), 1)),
        ],
        out_specs=pl.BlockSpec((None, ATTN_WIDTH, tq), lambda bi, i: (bi, 0, i)),
        out_shape=jax.ShapeDtypeStruct((b, ATTN_WIDTH, t), BF16),
        scratch_shapes=[pltpu.VMEM((2, 2 * WINDOW, Q_PER_KV * WINDOW), F32),
                        pltpu.VMEM((3, 2 * WINDOW, Q_PER_KV * WINDOW), F32)],
        compiler_params=_params(("parallel", "arbitrary")),
        name="swa",
    )(sinks, proj3, kv3, kv3, kv3, kv3)


def _mix_kernel(attn_ref, conv_ref, ga_ref, gg_ref, x_ref, wap_ref, wcp_ref, wmo_ref, o_ref,
                ya_ref, merged_ref, *, chunk):
    for c in range(o_ref.shape[1] // chunk):
        cols = slice(c * chunk, (c + 1) * chunk)
        ya_ref[:, cols] = lax.dot_general(attn_ref[...], wap_ref[:, cols], (((0,), (0,)), ((), ())),
                                          preferred_element_type=F32)
        y_attn = ya_ref[:, cols]
        y_conv = jnp.dot(conv_ref[...], wcp_ref[:, cols], preferred_element_type=F32)
        t_attn = jnp.tanh(ga_ref[:, cols].astype(F32))
        t_conv = jnp.tanh(gg_ref[:, cols].astype(F32))
        merged_ref[:, cols] = ((y_attn + y_conv) + (t_attn * y_attn + t_conv * y_conv)).astype(BF16)
    o_ref[...] = x_ref[...] + jnp.dot(merged_ref[...], wmo_ref[...], preferred_element_type=F32)


def _mix(attn_t, conv, proj, x2, wap, wcp, wmo, seq, *, tm, chunk):
    n, d = x2.shape
    tps = seq // tm
    gac, ggc = COL_GATE_A // d, COL_GATE_C // d
    const = lambda i: (0, 0)
    return pl.pallas_call(
        functools.partial(_mix_kernel, chunk=chunk),
        grid=(n // tm,),
        in_specs=[
            pl.BlockSpec((None, ATTN_WIDTH, tm), lambda i: (i // tps, 0, i % tps)),
            pl.BlockSpec((tm, conv.shape[1]), lambda i: (i, 0)),
            pl.BlockSpec((tm, d), lambda i: (i, gac)),
            pl.BlockSpec((tm, d), lambda i: (i, ggc)),
            pl.BlockSpec((tm, d), lambda i: (i, 0)),
            pl.BlockSpec(wap.shape, const, pipeline_mode=pl.Buffered(1)),
            pl.BlockSpec(wcp.shape, const, pipeline_mode=pl.Buffered(1)),
            pl.BlockSpec(wmo.shape, const, pipeline_mode=pl.Buffered(1)),
        ],
        out_specs=pl.BlockSpec((tm, d), lambda i: (i, 0)),
        out_shape=jax.ShapeDtypeStruct((n, d), F32),
        scratch_shapes=[pltpu.VMEM((tm, d), F32), pltpu.VMEM((tm, d), BF16)],
        compiler_params=_params(("parallel",)),
        name="mix",
    )(attn_t, conv, proj, proj, x2, wap, wcp, wmo)


def _memkv_kernel(m_ref, g_ref, w_ref, o_ref):
    u = _rms(m_ref[...], g_ref[...]).astype(BF16)
    o_ref[...] = jnp.dot(u, w_ref[...], preferred_element_type=F32).astype(o_ref.dtype)


def _memkv(mem2, g, w, *, tm):
    n, d = mem2.shape
    return pl.pallas_call(
        _memkv_kernel,
        grid=(n // tm,),
        in_specs=[
            pl.BlockSpec((tm, d), lambda i: (i, 0)),
            pl.BlockSpec((1, d), lambda i: (0, 0)),
            pl.BlockSpec(w.shape, lambda i: (0, 0)),
        ],
        out_specs=pl.BlockSpec((tm, w.shape[1]), lambda i: (i, 0)),
        out_shape=jax.ShapeDtypeStruct((n, w.shape[1]), BF16),
        compiler_params=_params(("parallel",)),
        name="memkv",
    )(mem2, g, w)


def _xattn_kernel(h_ref, g_ref, wq_ref, kv_ref, wo_ref, o_ref):
    h = h_ref[...]
    u = _rms(h, g_ref[...]).astype(BF16)
    q = jnp.dot(u, wq_ref[...], preferred_element_type=F32) * (X_HEAD_DIM ** -0.5)
    q = q.astype(BF16)
    xw = X_HEADS * X_HEAD_DIM
    outs = []
    for hd in range(X_HEADS):
        cols = slice(hd * X_HEAD_DIM, (hd + 1) * X_HEAD_DIM)
        kh = kv_ref[:, cols]
        vh = kv_ref[:, xw + hd * X_HEAD_DIM:xw + (hd + 1) * X_HEAD_DIM]
        s = lax.dot_general(q[:, cols], kh, (((1,), (1,)), ((), ())), preferred_element_type=F32)
        m = jnp.max(s, axis=-1, keepdims=True)
        p = jnp.exp(s - m)
        denom = jnp.sum(p, axis=-1, keepdims=True)
        outs.append((jnp.dot(p.astype(BF16), vh, preferred_element_type=F32) / denom).astype(BF16))
    o = jnp.concatenate(outs, axis=1)
    o_ref[...] = h + jnp.dot(o, wo_ref[...], preferred_element_type=F32)


def _xattn(h3, g, wq, kv3, wo, *, tm):
    b, t, d = h3.shape
    m, kvw = kv3.shape[1], kv3.shape[2]
    return pl.pallas_call(
        _xattn_kernel,
        grid=(b, t // tm),
        in_specs=[
            pl.BlockSpec((None, tm, d), lambda bi, i: (bi, i, 0)),
            pl.BlockSpec((1, d), lambda bi, i: (0, 0)),
            pl.BlockSpec(wq.shape, lambda bi, i: (0, 0)),
            pl.BlockSpec((None, m, kvw), lambda bi, i: (bi, 0, 0)),
            pl.BlockSpec(wo.shape, lambda bi, i: (0, 0)),
        ],
        out_specs=pl.BlockSpec((None, tm, d), lambda bi, i: (bi, i, 0)),
        out_shape=jax.ShapeDtypeStruct((b, t, d), F32),
        compiler_params=_params(("parallel", "parallel")),
        name="xattn",
    )(h3, g, wq, kv3, wo)


def _ffn_kernel(h_ref, g_ref, wa_ref, wb_ref, wo_ref, gf_ref, o_ref, u_ref, act_ref, *, sub, chunk):
    k = pl.program_id(1)
    last = pl.num_programs(1) - 1
    d = o_ref.shape[1]

    def hidden_tile(u, final_norm):
        th = act_ref.shape[1]
        for s in range(th // sub):
            hid = slice(s * sub, (s + 1) * sub)
            a = jnp.dot(u, wa_ref[:, hid], preferred_element_type=F32)
            b = jnp.dot(u, wb_ref[:, hid], preferred_element_type=F32)
            act_ref[:, hid] = (a * (1.0 + jnp.tanh(a)) * b).astype(BF16)
        sum_sq = jnp.zeros((o_ref.shape[0], 1), F32)
        for c in range(d // chunk):
            cols = slice(c * chunk, (c + 1) * chunk)
            y = o_ref[:, cols] + jnp.dot(act_ref[...], wo_ref[:, cols], preferred_element_type=F32)
            o_ref[:, cols] = y
            if final_norm:
                sum_sq = sum_sq + jnp.sum(y * y, axis=-1, keepdims=True)
        if final_norm:
            inv_rms = lax.rsqrt(sum_sq * (1.0 / d) + EPS)
            for c in range(d // chunk):
                cols = slice(c * chunk, (c + 1) * chunk)
                o_ref[:, cols] = o_ref[:, cols] * inv_rms * gf_ref[:, cols]

    @pl.when(k == 0)
    def _():
        h = h_ref[...]
        u = _rms(h, g_ref[...]).astype(BF16)
        u_ref[...] = u
        o_ref[...] = h
        hidden_tile(u, False)

    @pl.when((k > 0) & (k < last))
    def _():
        hidden_tile(u_ref[...], False)

    @pl.when(k == last)
    def _():
        hidden_tile(u_ref[...], True)


def _ffn(h2, g, w_in, w_out, g_final, *, tm, th, sub, chunk):
    n, d = h2.shape
    hidden = w_out.shape[0]
    nk = hidden // th
    assert nk >= 2
    return pl.pallas_call(
        functools.partial(_ffn_kernel, sub=sub, chunk=chunk),
        grid=(n // tm, nk),
        in_specs=[
            pl.BlockSpec((tm, d), lambda i, k: (i, 0)),
            pl.BlockSpec((1, d), lambda i, k: (0, 0)),
            pl.BlockSpec((d, th), lambda i, k: (0, k)),
            pl.BlockSpec((d, th), lambda i, k: (0, nk + k)),
            pl.BlockSpec((th, d), lambda i, k: (k, 0)),
            pl.BlockSpec((1, d), lambda i, k: (0, 0)),
        ],
        out_specs=pl.BlockSpec((tm, d), lambda i, k: (i, 0)),
        out_shape=jax.ShapeDtypeStruct((n, d), F32),
        scratch_shapes=[pltpu.VMEM((tm, d), BF16), pltpu.VMEM((tm, th), BF16)],
        compiler_params=_params(("parallel", "arbitrary"), BIG_VMEM_LIMIT),
        name="ffn",
    )(h2, g, w_in, w_in, w_out, g_final)


def _rope_tables(seq):
    half = HEAD_DIM // 2
    inv_freq = ROPE_THETA ** (-jnp.arange(half, dtype=F32) / half)
    ang = jnp.arange(seq, dtype=jnp.int32).astype(F32)[:, None] * inv_freq[None, :]
    cos, sin = jnp.cos(ang), jnp.sin(ang)
    reps = LANES // HEAD_DIM
    cos_t = jnp.tile(jnp.concatenate([cos, cos], axis=1), (1, reps))
    sin_t = jnp.tile(jnp.concatenate([-sin, sin], axis=1), (1, reps))
    return cos_t, sin_t


def _cast_w_in(w):
    col_scale = jnp.where(jnp.arange(w.shape[1]) >= REF_GATE_A, 0.5, 1.0).astype(F32)
    return (w * col_scale).astype(BF16)


def _col_scale(cols, value, lo=0, hi=None):
    idx = jnp.arange(cols)
    hi = cols if hi is None else hi
    return jnp.where((idx >= lo) & (idx < hi), value, 1.0).astype(F32)[None, :]


def kernel(x, mem, g_mix, w_in, conv_w, attn_sinks, w_attn_proj, w_conv_proj, w_mix_out, g_xattn, g_mem,
           w_xq, w_xkv, w_xo, g_ffn, w_ffn_in, w_ffn_out, g_final):
    b, t, d = x.shape
    depth = w_in.shape[0]
    assert depth == 1
    cos_t, sin_t = _rope_tables(t)
    h = x.reshape(b * t, d)
    mem2 = mem.reshape(b * mem.shape[1], d)
    for l in range(depth):
        w_in_bf = _cast_w_in(w_in[l])
        u, kv, conv = _norm_proj(h, g_mix[l][None, :], w_in_bf, conv_w[l], cos_t, sin_t, t, tm=1024, chunk=512)
        q_scale = HEAD_DIM ** -0.5 * LOG2E
        hidden = w_ffn_out.shape[1]
        side = [
            (w_ffn_in[l], _col_scale(2 * hidden, 0.5, 0, hidden), 64),
            (w_ffn_out[l], None, 176),
            (w_mix_out[l], _col_scale(d, 0.5), 64),
            (w_attn_proj[l], None, 32),
            (w_conv_proj[l], None, 32),
            (w_xq[l], None, 64),
            (w_xkv[l], None, 64),
            (w_xo[l], None, 16),
        ]
        proj, side_bf = _inproj(u, w_in_bf, cos_t * q_scale, sin_t * q_scale, t, side,
                                tm=1024, tn=MAIN_WIDTH // 2, chunk=512)
        (w_ffn_in_bf, w_ffn_out_bf, w_mix_out_bf, w_attn_proj_bf, w_conv_proj_bf,
         w_xq_bf, w_xkv_bf, w_xo_bf) = side_bf
        attn = _swa(proj.reshape(b, t, -1), kv.reshape(b, t, -1), attn_sinks[l], tq=1024)
        h = _mix(attn, conv, proj, h, w_attn_proj_bf, w_conv_proj_bf, w_mix_out_bf, t, tm=512, chunk=512)
        mkv = _memkv(mem2, g_mem[l][None, :], w_xkv_bf, tm=256)
        h = _xattn(h.reshape(b, t, d), g_xattn[l][None, :], w_xq_bf,
                   mkv.reshape(b, mem.shape[1], -1), w_xo_bf, tm=1024).reshape(b * t, d)
        h = _ffn(h, g_ffn[l][None, :], w_ffn_in_bf, w_ffn_out_bf,
                 g_final[None, :], tm=1024, th=512, sub=256, chunk=512)
    return h.reshape(b, t, d)
```
